```python
import math
import jax
import jax.numpy as jnp
from jax import lax
import numpy as np

D_MODEL = 1024
BATCH = 16
SEQ = 2048
DEPTH = 2
DEC_BATCH = 128
DEC_SEQ = 4
PAST_LEN = 16384
PAGE_SIZE = 128

N_PAGES = PAST_LEN // PAGE_SIZE
N_POOL_PAGES = (DEC_BATCH * N_PAGES * 5) // 4

MLA_HEADS = 8
MLA_NOPE = 64
MLA_ROPE = 32
MLA_V = 64
MLA_Q_LORA = 384
MLA_KV_LORA = 256
MLA_SCALE = (MLA_NOPE + MLA_ROPE) ** -0.5
ROPE_THETA = 10000.0
LRU_WIDTH = 512
LRU_BLOCKS = 8
LRU_BLOCK_W = LRU_WIDTH // LRU_BLOCKS
CONV_W = 4
LRU_C = 8.0
DSA_HEADS = 8
DSA_KV_HEADS = 2
DSA_GROUP = DSA_HEADS // DSA_KV_HEADS
DSA_HEAD_DIM = 64
DSA_SCALE = DSA_HEAD_DIM ** -0.5
IDX_HEADS = 4
IDX_DIM = 64
IDX_SCALE = IDX_DIM ** -0.5
DSA_TOPK_MAX = 256
N_BUCKETS = 32
MAX_DISTANCE = 128
N_EXPERTS = 32
TOP_K = 4
D_FF = 1024
SWIGLU_ALPHA = 1.702
SWIGLU_LIMIT = 7.0
MOE_BLOCK = 128
N_BRANCH = 3
QBLOCK = 128
EPS = 1e-6
F32 = jnp.float32

IN_SPLIT_SIZES = (MLA_Q_LORA, MLA_KV_LORA, MLA_ROPE, LRU_WIDTH,
                  DSA_HEADS * DSA_HEAD_DIM, DSA_KV_HEADS * DSA_HEAD_DIM, DSA_KV_HEADS * DSA_HEAD_DIM,
                  IDX_HEADS * IDX_DIM, IDX_DIM, IDX_HEADS, N_BRANCH * D_MODEL)
D_IN = sum(IN_SPLIT_SIZES)

kernel_name = 'hybrid_mla_rglru_dsa_moe_step'


def rmsnorm(x, g):
    xf = x.astype(F32)
    y = xf * lax.rsqrt(jnp.mean(xf * xf, axis=-1, keepdims=True) + EPS)
    return (y * g.astype(F32)).astype(x.dtype)


def modulate(x, g, shift, scale):
    return rmsnorm(x, g) * (1 + scale[:, None, :]) + shift[:, None, :]


def rope(x, pos):
    half = MLA_ROPE // 2
    freq = ROPE_THETA ** (-jnp.arange(half, dtype=F32) / half)
    ang = pos.astype(F32)[:, None] * freq[None, :]
    ang = ang.reshape((pos.shape[0],) + (1,) * (x.ndim - 3) + (half,))
    cos, sin = jnp.cos(ang), jnp.sin(ang)
    xf = x.astype(F32)
    x1, x2 = xf[..., :half], xf[..., half:]
    return jnp.concatenate([x1 * cos - x2 * sin, x1 * sin + x2 * cos], axis=-1).astype(x.dtype)


def t5_bucket(dist):
    n = jnp.maximum(dist, 0)
    max_exact = N_BUCKETS // 2
    nf = jnp.maximum(n, 1).astype(F32)
    large = max_exact + (jnp.log(nf / max_exact) / math.log(MAX_DISTANCE / max_exact)
                         * (N_BUCKETS - max_exact)).astype(jnp.int32)
    large = jnp.minimum(large, N_BUCKETS - 1)
    return jnp.where(n < max_exact, n, large)


def split_in(z):
    offsets = np.cumsum(IN_SPLIT_SIZES)[:-1].tolist()
    return jnp.split(z, offsets, axis=-1)


def to_qblocks(a):
    B, S = a.shape[:2]
    return a.reshape((B, S // QBLOCK, QBLOCK) + a.shape[2:]).swapaxes(0, 1)


def from_qblocks(a):
    nb, B, qb = a.shape[:3]
    return a.swapaxes(0, 1).reshape((B, nb * qb) + a.shape[3:])


def mixer_front(h, pos, lw):
    B, T = h.shape[:2]
    z = jnp.einsum('btd,de->bte', h, lw['w_in'])
    q_lat, kv_lat, kpe_raw, xr, q_c, k_c, v_c, q_i, k_i, w_i, g = split_in(z)
    cq = rmsnorm(q_lat, lw['g_q_norm'])
    q = jnp.einsum('btr,rhe->bthe', cq, lw['w_uq'])
    q_pe = rope(q[..., MLA_NOPE:], pos)
    q_abs = jnp.einsum('bthn,chn->bthc', q[..., :MLA_NOPE], lw['w_uk'])
    ckv = rmsnorm(kv_lat, lw['g_kv_norm'])
    kpe = rope(kpe_raw, pos)
    mla = (q_abs, q_pe, ckv, kpe)
    dsa = (q_c.reshape(B, T, DSA_HEADS, DSA_HEAD_DIM),
           k_c.reshape(B, T, DSA_KV_HEADS, DSA_HEAD_DIM),
           v_c.reshape(B, T, DSA_KV_HEADS, DSA_HEAD_DIM),
           q_i.reshape(B, T, IDX_HEADS, IDX_DIM), k_i, w_i)
    gates = jax.nn.sigmoid(g.astype(F32)).astype(h.dtype).reshape(B, T, N_BRANCH, D_MODEL)
    return mla, xr, dsa, gates


def mla_scores(q_abs, q_pe, ckv, kpe):
    s = (jnp.einsum('bqhc,bkc->bhqk', q_abs, ckv, preferred_element_type=F32)
         + jnp.einsum('bqhr,bkr->bhqk', q_pe, kpe, preferred_element_type=F32))
    return s * MLA_SCALE


def mla_prompt(q_abs, q_pe, ckv, kpe, w_uv):
    B, S = ckv.shape[:2]
    kpos = jnp.arange(S)

    def block(args):
        qa_b, qp_b, i = args
        qpos = i * QBLOCK + jnp.arange(QBLOCK)
        s = mla_scores(qa_b, qp_b, ckv, kpe)
        s = jnp.where(kpos[None, :] <= qpos[:, None], s, -jnp.inf)
        p = jax.nn.softmax(s, axis=-1).astype(ckv.dtype)
        return jnp.einsum('bhqk,bkc->bqhc', p, ckv)

    o_lat = from_qblocks(lax.map(block, (to_qblocks(q_abs), to_qblocks(q_pe), jnp.arange(S // QBLOCK))))
    o = jnp.einsum('bthc,chv->bthv', o_lat, w_uv)
    return o.reshape(B, S, MLA_HEADS * MLA_V)


def mla_sample(q_abs, q_pe, ckv_new, kpe_new, cache_lat, cache_kpe, layer, page_table, w_uv):
    Bd, T = ckv_new.shape[:2]

    def online(carry, s, v):
        m, l, acc = carry
        m_new = jnp.maximum(m, s.max(-1))
        alpha = jnp.exp(m - m_new)
        p = jnp.exp(s - m_new[..., None])
        return (m_new, l * alpha + p.sum(-1),
                acc * alpha[..., None] + jnp.einsum('bhqk,bkc->bhqc', p, v.astype(F32)))

    def page_step(carry, phys):
        lat = cache_lat[layer, phys]
        s = mla_scores(q_abs, q_pe, lat, cache_kpe[layer, phys])
        return online(carry, s, lat), None

    init = (jnp.full((Bd, MLA_HEADS, T), -jnp.inf, F32),
            jnp.zeros((Bd, MLA_HEADS, T), F32),
            jnp.zeros((Bd, MLA_HEADS, T, MLA_KV_LORA), F32))
    carry, _ = lax.scan(page_step, init, page_table.T)
    causal = jnp.tril(jnp.ones((T, T), dtype=bool))
    s_new = jnp.where(causal, mla_scores(q_abs, q_pe, ckv_new, kpe_new), -jnp.inf)
    m, l, acc = online(carry, s_new, ckv_new)
    o_lat = (acc / l[..., None]).astype(ckv_new.dtype)
    o = jnp.einsum('bhtc,chv->bthv', o_lat, w_uv)
    return o.reshape(Bd, T, MLA_HEADS * MLA_V)


def rglru(xr, conv_buf, h0, pos, lw):
    B, T, W = xr.shape
    xp = jnp.concatenate([conv_buf.astype(xr.dtype), xr], axis=1)
    xc = lw['b_conv'] + sum(xp[:, k:k + T] * lw['w_conv'][k] for k in range(CONV_W))
    xb = xc.reshape(B, T, LRU_BLOCKS, LRU_BLOCK_W)
    r = jax.nn.sigmoid((jnp.einsum('btnc,ncd->btnd', xb, lw['w_rg_a']).reshape(B, T, W) + lw['b_rg_a']).astype(F32))
    gi = jax.nn.sigmoid((jnp.einsum('btnc,ncd->btnd', xb, lw['w_rg_i']).reshape(B, T, W) + lw['b_rg_i']).astype(F32))
    log_a = -LRU_C * r * jax.nn.softplus(-lw['lru_lambda'].astype(F32))
    reset = (pos == 0)[None, :, None]
    a = jnp.where(reset, 0.0, jnp.exp(log_a))
    mult = jnp.where(reset, 1.0, jnp.sqrt(-jnp.expm1(2.0 * log_a)))
    u = xc.astype(F32) * gi * mult

    def step(hc, au):
        a_t, u_t = au
        hc = a_t * hc + u_t
        return hc, hc

    hT, hs = lax.scan(step, h0.astype(F32), (a.swapaxes(0, 1), u.swapaxes(0, 1)))
    return hs.swapaxes(0, 1).astype(xr.dtype), hT.astype(h0.dtype), xp[:, T:]


def index_scores(q_i, w_i, k_i):
    dots = jnp.einsum('bqhd,bkd->bqhk', q_i, k_i, preferred_element_type=F32) * IDX_SCALE
    return jnp.einsum('bqh,bqhk->bqk', w_i.astype(F32) * IDX_HEADS ** -0.5, jax.nn.relu(dots))


def sparse_attend(q, k_sel, v_sel, dist, valid, rel_bias):
    B, T = q.shape[:2]
    n = k_sel.shape[2]
    qg = q.reshape(B, T, DSA_KV_HEADS, DSA_GROUP, DSA_HEAD_DIM)
    s = jnp.einsum('btjgd,btnjd->btjgn', qg, k_sel, preferred_element_type=F32) * DSA_SCALE
    bias = rel_bias.astype(F32)[t5_bucket(dist)]
    bias = bias.reshape(B, T, n, DSA_KV_HEADS, DSA_GROUP).transpose(0, 1, 3, 4, 2)
    s = jnp.where(valid[:, :, None, None, :], s + bias, -jnp.inf)
    p = jax.nn.softmax(s, axis=-1).astype(v_sel.dtype)
    o = jnp.einsum('btjgn,btnjd->btjgd', p, v_sel)
    return o.reshape(B, T, DSA_HEADS * DSA_HEAD_DIM)


def dsa_prompt(q, k, v, q_i, w_i, k_i, rel_bias):
    B, S = q.shape[:2]
    n_sel = min(DSA_TOPK_MAX, S // 4)
    kpos = jnp.arange(S)
    bidx = jnp.arange(B)[:, None, None]

    def block(args):
        q_b, qi_b, wi_b, i = args
        qpos = i * QBLOCK + jnp.arange(QBLOCK)
        score = jnp.where(kpos[None, :] <= qpos[:, None], index_scores(qi_b, wi_b, k_i), -jnp.inf)
        _, top_i = lax.top_k(score, n_sel)
        valid = top_i <= qpos[None, :, None]
        dist = qpos[None, :, None] - top_i
        return sparse_attend(q_b, k[bidx, top_i], v[bidx, top_i], dist, valid, rel_bias)

    o = lax.map(block, (to_qblocks(q), to_qblocks(q_i), to_qblocks(w_i), jnp.arange(S // QBLOCK)))
    return from_qblocks(o)


def dsa_sample(q, k_new, v_new, q_i, w_i, k_i_new, cache_k, cache_v, cache_ki, layer, page_table, rel_bias):
    Bd, T = q.shape[:2]
    L = PAST_LEN + T
    n_sel = min(DSA_TOPK_MAX, L // 4)
    ki_past = cache_ki[layer, page_table].reshape(Bd, PAST_LEN, IDX_DIM)
    ki_all = jnp.concatenate([ki_past.astype(k_i_new.dtype), k_i_new], axis=1)
    qpos = PAST_LEN + jnp.arange(T)
    kpos = jnp.arange(L)
    score = jnp.where(kpos[None, :] <= qpos[:, None], index_scores(q_i, w_i, ki_all), -jnp.inf)
    _, top_i = lax.top_k(score, n_sel)
    bidx = jnp.arange(Bd)[:, None, None]
    in_past = (top_i < PAST_LEN)[..., None, None]
    pi = jnp.minimum(top_i, PAST_LEN - 1)
    page = page_table[bidx, pi // PAGE_SIZE]
    row = pi % PAGE_SIZE
    ni = jnp.clip(top_i - PAST_LEN, 0, T - 1)
    k_sel = jnp.where(in_past, cache_k[layer, page, row].astype(k_new.dtype), k_new[bidx, ni])
    v_sel = jnp.where(in_past, cache_v[layer, page, row].astype(v_new.dtype), v_new[bidx, ni])
    valid = top_i <= qpos[None, :, None]
    dist = qpos[None, :, None] - top_i
    return sparse_attend(q, k_sel, v_sel, dist, valid, rel_bias)


def merge_branches(o_mla, o_lru, o_dsa, gates, lw):
    y = (gates[:, :, 0] * (o_mla @ lw['w_br_mla'])
         + gates[:, :, 1] * (o_lru @ lw['w_br_lru'])
         + gates[:, :, 2] * (o_dsa @ lw['w_br_dsa']))
    return y @ lw['w_o']


def mixer_prompt(h, lw, rel_bias):
    B, S = h.shape[:2]
    pos = jnp.arange(S)
    (q_abs, q_pe, ckv, kpe), xr, (q_c, k_c, v_c, q_i, k_i, w_i), gates = mixer_front(h, pos, lw)
    o_mla = mla_prompt(q_abs, q_pe, ckv, kpe, lw['w_uv'])
    conv0 = jnp.zeros((B, CONV_W - 1, LRU_WIDTH), h.dtype)
    h0 = jnp.zeros((B, LRU_WIDTH), h.dtype)
    o_lru, hT, conv_buf = rglru(xr, conv0, h0, pos, lw)
    o_dsa = dsa_prompt(q_c, k_c, v_c, q_i, w_i, k_i, rel_bias)
    y = merge_branches(o_mla, o_lru, o_dsa, gates, lw)
    return y, (ckv, kpe, k_c, v_c, k_i, hT, conv_buf)


def mixer_sample(h, layer, lw, rel_bias, cache_mla_latent, cache_mla_kpe, cache_dsa_k, cache_dsa_v,
                 cache_idx_k, state_lru_h, state_conv, page_table):
    T = h.shape[1]
    pos = PAST_LEN + jnp.arange(T)
    (q_abs, q_pe, ckv, kpe), xr, (q_c, k_c, v_c, q_i, k_i, w_i), gates = mixer_front(h, pos, lw)
    o_mla = mla_sample(q_abs, q_pe, ckv, kpe, cache_mla_latent, cache_mla_kpe, layer, page_table, lw['w_uv'])
    o_lru, hT, conv_buf = rglru(xr, state_conv[layer], state_lru_h[layer], pos, lw)
    o_dsa = dsa_sample(q_c, k_c, v_c, q_i, w_i, k_i, cache_dsa_k, cache_dsa_v, cache_idx_k,
                       layer, page_table, rel_bias)
    y = merge_branches(o_mla, o_lru, o_dsa, gates, lw)
    return y, (ckv, kpe, k_c, v_c, k_i, hT, conv_buf)


def swiglu_clamped(z):
    glu, lin = z[..., :D_FF], z[..., D_FF:]
    glu = jnp.minimum(glu, SWIGLU_LIMIT)
    lin = jnp.clip(lin, -SWIGLU_LIMIT, SWIGLU_LIMIT)
    return glu * jax.nn.sigmoid(SWIGLU_ALPHA * glu) * (lin + 1)


def moe(h, w_router, b_router, w_up, b_up, w_down, b_down):
    shp = h.shape
    x = h.reshape(-1, D_MODEL)
    N = x.shape[0]
    logits = jnp.dot(x, w_router, preferred_element_type=F32) + b_router.astype(F32)
    top_v, top_e = lax.top_k(logits, TOP_K)
    gate = jax.nn.softmax(top_v, axis=-1)
    M = N * TOP_K
    flat_e = top_e.reshape(M)
    order = jnp.argsort(flat_e)
    e_sorted = flat_e[order]
    tok = order // TOP_K
    counts = jnp.bincount(flat_e, length=N_EXPERTS)
    padded = (counts + MOE_BLOCK - 1) // MOE_BLOCK * MOE_BLOCK
    pad_end = jnp.cumsum(padded)
    pad_start = pad_end - padded
    start = jnp.cumsum(counts) - counts
    dest = pad_start[e_sorted] + jnp.arange(M) - start[e_sorted]
    n_blocks = -(-M // MOE_BLOCK) + N_EXPERTS
    rows = jnp.zeros((n_blocks * MOE_BLOCK, D_MODEL), x.dtype).at[dest].set(x[tok])
    blk_e = jnp.minimum(jnp.searchsorted(pad_end, jnp.arange(n_blocks) * MOE_BLOCK, side='right'),
                        N_EXPERTS - 1)

    def expert_block(args):
        r, e = args
        z = r @ w_up[e] + b_up[e]
        return swiglu_clamped(z) @ w_down[e] + b_down[e]

    out = lax.map(expert_block, (rows.reshape(n_blocks, MOE_BLOCK, D_MODEL), blk_e)).reshape(-1, D_MODEL)
    contrib = out[dest] * gate.reshape(M)[order][:, None].astype(out.dtype)
    y = jnp.zeros_like(x).at[tok].add(contrib)
    return y.reshape(shp)


def setup_inputs(seed: int = 0) -> dict:
    key = jax.random.key(seed)
    keys = iter(jax.random.split(key, 64))

    def nrm(shape, scale):
        return jax.random.normal(next(keys), shape, F32) * scale

    D = D_MODEL
    x_prompt = nrm((BATCH, SEQ, D), 1.0)
    x_sample = nrm((DEC_BATCH, DEC_SEQ, D), 1.0)
    cache_mla_latent = nrm((DEPTH, N_POOL_PAGES, PAGE_SIZE, MLA_KV_LORA), 1.0)
    cache_mla_kpe = nrm((DEPTH, N_POOL_PAGES, PAGE_SIZE, MLA_ROPE), 1.0)
    cache_dsa_k = nrm((DEPTH, N_POOL_PAGES, PAGE_SIZE, DSA_KV_HEADS, DSA_HEAD_DIM), 1.0)
    cache_dsa_v = nrm((DEPTH, N_POOL_PAGES, PAGE_SIZE, DSA_KV_HEADS, DSA_HEAD_DIM), 1.0)
    cache_idx_k = nrm((DEPTH, N_POOL_PAGES, PAGE_SIZE, IDX_DIM), 1.0)
    state_lru_h = nrm((DEPTH, DEC_BATCH, LRU_WIDTH), 0.5)
    state_conv = nrm((DEPTH, DEC_BATCH, CONV_W - 1, LRU_WIDTH), 1.0)
    page_table = jax.random.permutation(next(keys), N_POOL_PAGES)[:DEC_BATCH * N_PAGES]
    page_table = page_table.reshape(DEC_BATCH, N_PAGES).astype(jnp.int32)
    c_prompt = nrm((BATCH, D), 1.0)
    c_sample = nrm((DEC_BATCH, D), 1.0)
    rel_bias = nrm((N_BUCKETS, DSA_HEADS), 0.5)
    g_norm_mix = 1.0 + nrm((DEPTH, D), 0.02)
    g_norm_ffn = 1.0 + nrm((DEPTH, D), 0.02)
    w_ada = nrm((DEPTH, D, 6 * D), 0.5 * D ** -0.5)
    b_ada = nrm((DEPTH, 6 * D), 0.02)
    w_in = nrm((DEPTH, D, D_IN), D ** -0.5)
    g_q_norm = 1.0 + nrm((DEPTH, MLA_Q_LORA), 0.02)
    w_uq = nrm((DEPTH, MLA_Q_LORA, MLA_HEADS, MLA_NOPE + MLA_ROPE), MLA_Q_LORA ** -0.5)
    g_kv_norm = 1.0 + nrm((DEPTH, MLA_KV_LORA), 0.02)
    w_uk = nrm((DEPTH, MLA_KV_LORA, MLA_HEADS, MLA_NOPE), MLA_KV_LORA ** -0.5)
    w_uv = nrm((DEPTH, MLA_KV_LORA, MLA_HEADS, MLA_V), MLA_KV_LORA ** -0.5)
    w_conv = nrm((DEPTH, CONV_W, LRU_WIDTH), CONV_W ** -0.5)
    b_conv = nrm((DEPTH, LRU_WIDTH), 0.02)
    w_rg_a = nrm((DEPTH, LRU_BLOCKS, LRU_BLOCK_W, LRU_BLOCK_W), LRU_BLOCK_W ** -0.5)
    b_rg_a = nrm((DEPTH, LRU_WIDTH), 0.02)
    w_rg_i = nrm((DEPTH, LRU_BLOCKS, LRU_BLOCK_W, LRU_BLOCK_W), LRU_BLOCK_W ** -0.5)
    b_rg_i = nrm((DEPTH, LRU_WIDTH), 0.02)
    u = jax.random.uniform(next(keys), (DEPTH, LRU_WIDTH), F32, 0.9, 0.999)
    s = u ** (1.0 / LRU_C)
    lru_lambda = jnp.log(s) - jnp.log1p(-s)
    w_br_mla = nrm((DEPTH, MLA_HEADS * MLA_V, D), (MLA_HEADS * MLA_V) ** -0.5)
    w_br_lru = nrm((DEPTH, LRU_WIDTH, D), LRU_WIDTH ** -0.5)
    w_br_dsa = nrm((DEPTH, DSA_HEADS * DSA_HEAD_DIM, D), (DSA_HEADS * DSA_HEAD_DIM) ** -0.5)
    w_o = nrm((DEPTH, D, D), D ** -0.5)
    w_router = nrm((DEPTH, D, N_EXPERTS), D ** -0.5)
    b_router = nrm((DEPTH, N_EXPERTS), 0.01)
    w_up = nrm((DEPTH, N_EXPERTS, D, 2 * D_FF), D ** -0.5)
    b_up = nrm((DEPTH, N_EXPERTS, 2 * D_FF), 0.02)
    w_down = nrm((DEPTH, N_EXPERTS, D_FF, D), D_FF ** -0.5)
    b_down = nrm((DEPTH, N_EXPERTS, D), 0.02)
    g_final = 1.0 + nrm((D,), 0.02)
    return {'x_prompt': x_prompt, 'x_sample': x_sample,
            'cache_mla_latent': cache_mla_latent, 'cache_mla_kpe': cache_mla_kpe,
            'cache_dsa_k': cache_dsa_k, 'cache_dsa_v': cache_dsa_v, 'cache_idx_k': cache_idx_k,
            'state_lru_h': state_lru_h, 'state_conv': state_conv, 'page_table': page_table,
            'c_prompt': c_prompt, 'c_sample': c_sample, 'rel_bias': rel_bias,
            'g_norm_mix': g_norm_mix, 'g_norm_ffn': g_norm_ffn, 'w_ada': w_ada, 'b_ada': b_ada,
            'w_in': w_in, 'g_q_norm': g_q_norm, 'w_uq': w_uq, 'g_kv_norm': g_kv_norm,
            'w_uk': w_uk, 'w_uv': w_uv, 'w_conv': w_conv, 'b_conv': b_conv,
            'w_rg_a': w_rg_a, 'b_rg_a': b_rg_a, 'w_rg_i': w_rg_i, 'b_rg_i': b_rg_i,
            'lru_lambda': lru_lambda, 'w_br_mla': w_br_mla, 'w_br_lru': w_br_lru,
            'w_br_dsa': w_br_dsa, 'w_o': w_o, 'w_router': w_router, 'b_router': b_router,
            'w_up': w_up, 'b_up': b_up, 'w_down': w_down, 'b_down': b_down, 'g_final': g_final}


def reference(x_prompt, x_sample, cache_mla_latent, cache_mla_kpe, cache_dsa_k, cache_dsa_v, cache_idx_k,
              state_lru_h, state_conv, page_table, c_prompt, c_sample, rel_bias,
              g_norm_mix, g_norm_ffn, w_ada, b_ada, w_in, g_q_norm, w_uq, g_kv_norm, w_uk, w_uv,
              w_conv, b_conv, w_rg_a, b_rg_a, w_rg_i, b_rg_i, lru_lambda,
              w_br_mla, w_br_lru, w_br_dsa, w_o, w_router, b_router, w_up, b_up, w_down, b_down, g_final):
    xp, xs = x_prompt, x_sample
    new_p = [[] for _ in range(7)]
    new_s = [[] for _ in range(7)]
    for l in range(DEPTH):
        lw = {'w_in': w_in[l], 'g_q_norm': g_q_norm[l], 'w_uq': w_uq[l], 'g_kv_norm': g_kv_norm[l],
              'w_uk': w_uk[l], 'w_uv': w_uv[l], 'w_conv': w_conv[l], 'b_conv': b_conv[l],
              'w_rg_a': w_rg_a[l], 'b_rg_a': b_rg_a[l], 'w_rg_i': w_rg_i[l], 'b_rg_i': b_rg_i[l],
              'lru_lambda': lru_lambda[l], 'w_br_mla': w_br_mla[l], 'w_br_lru': w_br_lru[l],
              'w_br_dsa': w_br_dsa[l], 'w_o': w_o[l]}
        ada_p = jnp.split(c_prompt @ w_ada[l] + b_ada[l], 6, axis=-1)
        ada_s = jnp.split(c_sample @ w_ada[l] + b_ada[l], 6, axis=-1)
        mix_p, st_p = mixer_prompt(modulate(xp, g_norm_mix[l], ada_p[0], ada_p[1]), lw, rel_bias)
        xp = xp + ada_p[2][:, None, :] * mix_p
        mix_s, st_s = mixer_sample(modulate(xs, g_norm_mix[l], ada_s[0], ada_s[1]), l, lw, rel_bias,
                                   cache_mla_latent, cache_mla_kpe, cache_dsa_k, cache_dsa_v,
                                   cache_idx_k, state_lru_h, state_conv, page_table)
        xs = xs + ada_s[2][:, None, :] * mix_s
        xp = xp + ada_p[5][:, None, :] * moe(modulate(xp, g_norm_ffn[l], ada_p[3], ada_p[4]),
                                             w_router[l], b_router[l], w_up[l], b_up[l], w_down[l], b_down[l])
        xs = xs + ada_s[5][:, None, :] * moe(modulate(xs, g_norm_ffn[l], ada_s[3], ada_s[4]),
                                             w_router[l], b_router[l], w_up[l], b_up[l], w_down[l], b_down[l])
        for i in range(7):
            new_p[i].append(st_p[i])
            new_s[i].append(st_s[i])
    y_prompt = rmsnorm(xp, g_final)
    y_sample = rmsnorm(xs, g_final)
    new_mla_latent_p = jnp.stack(new_p[0])
    new_mla_kpe_p = jnp.stack(new_p[1])
    new_dsa_k_p = jnp.stack(new_p[2])
    new_dsa_v_p = jnp.stack(new_p[3])
    new_idx_k_p = jnp.stack(new_p[4])
    new_lru_h_p = jnp.stack(new_p[5])
    new_conv_p = jnp.stack(new_p[6])
    new_mla_latent_s = jnp.stack(new_s[0])
    new_mla_kpe_s = jnp.stack(new_s[1])
    new_dsa_k_s = jnp.stack(new_s[2])
    new_dsa_v_s = jnp.stack(new_s[3])
    new_idx_k_s = jnp.stack(new_s[4])
    new_lru_h_s = jnp.stack(new_s[5])
    new_conv_s = jnp.stack(new_s[6])
    return (y_prompt, y_sample,
            new_mla_latent_p, new_mla_kpe_p, new_dsa_k_p, new_dsa_v_p, new_idx_k_p, new_lru_h_p, new_conv_p,
            new_mla_latent_s, new_mla_kpe_s, new_dsa_k_s, new_dsa_v_s, new_idx_k_s, new_lru_h_s, new_conv_s)
```

```python
import functools
import math

import numpy as np
import jax
import jax.numpy as jnp
from jax import lax
from jax.experimental import pallas as pl
from jax.experimental.pallas import tpu as pltpu

F32 = jnp.float32
BF16 = jnp.bfloat16
I32 = jnp.int32

EPS = 1e-6
MLA_HEADS = 8
MLA_NOPE = 64
MLA_ROPE = 32
MLA_V = 64
MLA_Q_LORA = 384
MLA_KV_LORA = 256
MLA_SCALE = (MLA_NOPE + MLA_ROPE) ** -0.5
ROPE_THETA = 10000.0
LRU_WIDTH = 512
LRU_BLOCKS = 8
CONV_W = 4
LRU_C = 8.0
DSA_HEADS = 8
DSA_KV_HEADS = 2
DSA_GROUP = DSA_HEADS // DSA_KV_HEADS
DSA_HEAD_DIM = 64
DSA_SCALE = DSA_HEAD_DIM ** -0.5
IDX_HEADS = 4
IDX_DIM = 64
IDX_SCALE = IDX_DIM ** -0.5
DSA_TOPK_MAX = 256
N_BUCKETS = 32
MAX_DISTANCE = 128
N_EXPERTS = 32
TOP_K = 4
SWIGLU_ALPHA = 1.702
SWIGLU_LIMIT = 7.0
N_BRANCH = 3

LANES = 128
ROPE_SLOT = LANES // MLA_ROPE
QK_W = MLA_KV_LORA + LANES
VMEM_LIMIT = 56 * 1024 * 1024
INT_MIN = -2 ** 31
KEY_NEG = int(np.array(-np.inf, np.float32).view(np.int32)) ^ 0x7FFFFFFF
NEG_INF = float("-inf")

ROW_TILE = 256
MLA_TQ = 256
MLA_TK = 256
DSA_TQ = 128
DSA_TK = 256
MOE_BLOCK = 256
SAMPLE_TK = 2048


def _cp(*sem):
    return pltpu.CompilerParams(dimension_semantics=sem, vmem_limit_bytes=VMEM_LIMIT)


def _whole(shape):
    nd = len(shape)
    return pl.BlockSpec(shape, lambda *a: (0,) * nd)


def _dot(a, b):
    return jnp.dot(a, b, preferred_element_type=F32)


def _dot_nt(a, b):
    return lax.dot_general(a, b, (((1,), (1,)), ((), ())), preferred_element_type=F32)


def _rms(x, g):
    return x * lax.rsqrt(jnp.mean(x * x, axis=-1, keepdims=True) + EPS) * g


def _mm_kernel(x_ref, w_ref, b_ref, o_ref):
    o_ref[...] = _dot(x_ref[...].astype(BF16), w_ref[...].astype(BF16)) + b_ref[...]


def _matmul(x, w, b, tn, name):
    M, K = x.shape
    N = w.shape[1]
    return pl.pallas_call(
        _mm_kernel,
        out_shape=jax.ShapeDtypeStruct((M, N), F32),
        grid=(N // tn,),
        in_specs=[pl.BlockSpec((M, K), lambda j: (0, 0)),
                  pl.BlockSpec((K, tn), lambda j: (0, j)),
                  pl.BlockSpec((1, tn), lambda j: (0, j))],
        out_specs=pl.BlockSpec((M, tn), lambda j: (0, j)),
        compiler_params=_cp("parallel"),
        name=name,
    )(x, w, b)


def _bucket_starts():
    n = np.arange(0, 8 * MAX_DISTANCE)
    max_exact = N_BUCKETS // 2

    def buckets(dt):
        nf = np.maximum(n, 1).astype(dt)
        large = max_exact + (np.log(nf / dt(max_exact)) / dt(math.log(MAX_DISTANCE / max_exact))
                             * dt(N_BUCKETS - max_exact)).astype(np.int32)
        return np.where(n < max_exact, n, np.minimum(large, N_BUCKETS - 1))

    b64 = buckets(np.float64)
    assert (buckets(np.float32) == b64).all() and (np.diff(b64) >= 0).all()
    starts = [int(np.argmax(b64 >= k)) if (b64 >= k).any() else int(n[-1]) + 1 for k in range(N_BUCKETS)]
    far = int(np.argmax(b64 == N_BUCKETS - 1))
    assert (b64[far:] == N_BUCKETS - 1).all()
    return starts, far


_BUCKET_STARTS, _BUCKET_FAR = _bucket_starts()


def _bias_kernel(tab_ref, o_ref, *, delta0, delta_step, row_mult):
    i = pl.program_id(0)
    R, C = o_ref.shape[2:]
    r = lax.broadcasted_iota(I32, (R, C), 0)
    c = lax.broadcasted_iota(I32, (R, C), 1)
    dist = delta0 + delta_step * i + row_mult * r - c
    for h in range(DSA_HEADS):
        val = jnp.full((R, C), tab_ref[0, h], F32)
        for k in range(1, N_BUCKETS):
            val = jnp.where(dist >= _BUCKET_STARTS[k], tab_ref[k, h], val)
        o_ref[0, h] = val


def _bias_tiles(rel_bias, n, R, C, delta0, delta_step, row_mult, name):
    return pl.pallas_call(
        functools.partial(_bias_kernel, delta0=delta0, delta_step=delta_step, row_mult=row_mult),
        out_shape=jax.ShapeDtypeStruct((n, DSA_HEADS, R, C), F32),
        grid=(n,),
        in_specs=[pl.BlockSpec(memory_space=pltpu.SMEM)],
        out_specs=pl.BlockSpec((1, DSA_HEADS, R, C), lambda i: (i, 0, 0, 0)),
        compiler_params=_cp("parallel"),
        name=name,
    )(rel_bias)


_C_QLAT = 0
_C_KV = _C_QLAT + MLA_Q_LORA
_C_KPE = _C_KV + MLA_KV_LORA
_C_KPES = _C_KPE + LANES
_C_XR = _C_KPES + LANES
_C_QC = _C_XR + LRU_WIDTH
_C_KC = _C_QC + DSA_HEADS * DSA_HEAD_DIM
_C_VC = _C_KC + LANES
_C_QI = _C_VC + LANES
_C_KI = _C_QI + IDX_HEADS * IDX_DIM
_C_WI = _C_KI + LANES
_C_G = _C_WI + LANES


def _rep2(x, lo_mask):
    xr = pltpu.roll(x, 64, axis=1)
    return jnp.where(lo_mask, x, xr), jnp.where(lo_mask, xr, x)


def _inproj_kernel(x_ref, gn_ref, sh_ref, sc_ref, cos_ref, sin_ref, w_ref, gq_ref, wn_ref, wuk_ref,
                   wr_ref, wrs_ref, gkv_ref,
                   q_out, ckv_out, kp_out, kpe_out, xr_out, qc_out, kc_out, vc_out, krep_out, vrep_out,
                   qi_out, ki_out, kirep_out, wi_out, gates_out):
    D = x_ref.shape[1]
    tm = x_ref.shape[0]
    hb = (_rms(x_ref[...], gn_ref[...]) * (1.0 + sc_ref[...]) + sh_ref[...]).astype(BF16)

    def seg(lo, width):
        return _dot(hb, w_ref[:, lo:lo + width])

    cos = cos_ref[...]
    sin = sin_ref[...]
    lane = lax.broadcasted_iota(I32, (tm, LANES), 1)
    lo_mask = lane < 64

    cqb = _rms(seg(_C_QLAT, MLA_Q_LORA), gq_ref[...]).astype(BF16)
    qpe = (_dot(cqb, wr_ref[...]) * cos + _dot(cqb, wrs_ref[...]) * sin) * MLA_SCALE
    for h in range(MLA_HEADS):
        qn = _dot(cqb, wn_ref[h]).astype(BF16)
        q_out[h, :, 0:MLA_KV_LORA] = (_dot(qn, wuk_ref[h]) * MLA_SCALE).astype(BF16)
        blk = qpe[:, (h // ROPE_SLOT) * LANES:(h // ROPE_SLOT + 1) * LANES]
        slot = h % ROPE_SLOT
        keep = (lane >= slot * MLA_ROPE) & (lane < (slot + 1) * MLA_ROPE)
        q_out[h, :, MLA_KV_LORA:QK_W] = jnp.where(keep, blk, 0.0).astype(BF16)

    ckv = _rms(seg(_C_KV, MLA_KV_LORA), gkv_ref[...])
    ckv_out[...] = ckv
    kpe = seg(_C_KPE, LANES) * cos[:, :LANES] + seg(_C_KPES, LANES) * sin[:, :LANES]
    kpe_out[...] = kpe
    kp_out[:, 0:MLA_KV_LORA] = ckv.astype(BF16)
    kp_out[:, MLA_KV_LORA:QK_W] = kpe.astype(BF16)

    xr_out[...] = seg(_C_XR, LRU_WIDTH)

    qc_out[...] = (seg(_C_QC, DSA_HEADS * DSA_HEAD_DIM) * DSA_SCALE).astype(BF16)
    for c0, f32_out, rep_out in ((_C_KC, kc_out, krep_out), (_C_VC, vc_out, vrep_out)):
        kv = seg(c0, LANES)
        f32_out[...] = kv
        g0, g1 = _rep2(kv, lo_mask)
        g0 = g0.astype(BF16)
        g1 = g1.astype(BF16)
        rep_out[:, 0:128] = g0
        rep_out[:, 128:256] = g0
        rep_out[:, 256:384] = g1
        rep_out[:, 384:512] = g1
    qi_out[...] = (seg(_C_QI, IDX_HEADS * IDX_DIM) * IDX_SCALE).astype(BF16)
    ki = seg(_C_KI, LANES)
    ki_out[...] = ki
    ki2 = (ki + pltpu.roll(ki, 64, axis=1)).astype(BF16)
    kirep_out[:, 0:128] = ki2
    kirep_out[:, 128:256] = ki2
    wi_out[...] = seg(_C_WI, LANES) * (IDX_HEADS ** -0.5)

    for j in range(N_BRANCH):
        gates_out[:, j * D:(j + 1) * D] = jax.nn.sigmoid(seg(_C_G + j * D, D)).astype(BF16)


def _in_proj(x, shift, scale, group_rows, cos, sin, lw):
    N, D = x.shape
    tm = ROW_TILE
    R = shift.shape[1]
    tiles_per_group = group_rows // tm
    ptiles = cos.shape[0] // tm
    row = lambda w: pl.BlockSpec((tm, w), lambda i: (i, 0))
    mod = pl.BlockSpec((None, R, D), lambda i: (i // tiles_per_group, 0, 0))
    tab = pl.BlockSpec((tm, 2 * LANES), lambda i: (i % ptiles, 0))
    outs = [
        (jax.ShapeDtypeStruct((MLA_HEADS, N, QK_W), BF16), pl.BlockSpec((MLA_HEADS, tm, QK_W), lambda i: (0, i, 0))),
        (jax.ShapeDtypeStruct((N, MLA_KV_LORA), F32), row(MLA_KV_LORA)),
        (jax.ShapeDtypeStruct((N, QK_W), BF16), row(QK_W)),
        (jax.ShapeDtypeStruct((N, LANES), F32), row(LANES)),
        (jax.ShapeDtypeStruct((N, LRU_WIDTH), F32), row(LRU_WIDTH)),
        (jax.ShapeDtypeStruct((N, 512), BF16), row(512)),
        (jax.ShapeDtypeStruct((N, LANES), F32), row(LANES)),
        (jax.ShapeDtypeStruct((N, LANES), F32), row(LANES)),
        (jax.ShapeDtypeStruct((N, 512), BF16), row(512)),
        (jax.ShapeDtypeStruct((N, 512), BF16), row(512)),
        (jax.ShapeDtypeStruct((N, 256), BF16), row(256)),
        (jax.ShapeDtypeStruct((N, LANES), F32), row(LANES)),
        (jax.ShapeDtypeStruct((N, 256), BF16), row(256)),
        (jax.ShapeDtypeStruct((N, LANES), F32), row(LANES)),
        (jax.ShapeDtypeStruct((N, N_BRANCH * D), BF16), row(N_BRANCH * D)),
    ]
    return pl.pallas_call(
        _inproj_kernel,
        out_shape=[o[0] for o in outs],
        grid=(N // tm,),
        in_specs=[row(D), _whole((1, D)), mod, mod, tab, tab,
                  _whole(lw["w_in"].shape), _whole((1, MLA_Q_LORA)), _whole(lw["wn"].shape),
                  _whole(lw["wuk"].shape), _whole(lw["wr"].shape), _whole(lw["wrs"].shape),
                  _whole((1, MLA_KV_LORA))],
        out_specs=[o[1] for o in outs],
        compiler_params=_cp("parallel"),
        name="in_proj",
    )(x, lw["g_norm_mix"], shift, scale, cos, sin, lw["w_in"], lw["g_q_norm"], lw["wn"], lw["wuk"],
      lw["wr"], lw["wrs"], lw["g_kv_norm"])


def _mla_prompt_kernel(qi_tab, ki_tab, q_ref, k_ref, bd_ref, o_ref, m_ref, l_ref, acc_ref, *, tq, tk):
    step = pl.program_id(1)
    qi = qi_tab[step]
    ki = ki_tab[step]
    last = (qi * tq + tq - 1) // tk

    @pl.when(ki == 0)
    def _():
        m_ref[...] = jnp.full(m_ref.shape, NEG_INF, F32)
        l_ref[...] = jnp.zeros(l_ref.shape, F32)
        acc_ref[...] = jnp.zeros(acc_ref.shape, F32)

    H = MLA_HEADS
    k = k_ref[...]
    s = _dot_nt(q_ref[...].reshape(H * tq, QK_W), k).reshape(H, tq, tk)
    qpos = qi * tq + lax.broadcasted_iota(I32, (tq, tk), 0)
    kpos = ki * tk + lax.broadcasted_iota(I32, (tq, tk), 1)
    s = jnp.where((kpos <= qpos)[None], s, NEG_INF)
    m_old = m_ref[...]
    m_new = jnp.maximum(m_old, jnp.max(s, axis=-1, keepdims=True))
    alpha = jnp.exp(m_old - m_new)
    p = jnp.exp(s - m_new)
    l_ref[...] = alpha * l_ref[...] + jnp.sum(p, axis=-1, keepdims=True)
    pv = _dot(p.reshape(H * tq, tk).astype(BF16), k[:, 0:MLA_KV_LORA]).reshape(H, tq, MLA_KV_LORA)
    acc_ref[...] = alpha * acc_ref[...] + pv
    m_ref[...] = m_new

    @pl.when(ki == last)
    def _():
        o = acc_ref[...] / l_ref[...]
        for pr in range(H // 2):
            cat = jnp.concatenate([o[2 * pr], o[2 * pr + 1]], axis=-1).astype(BF16)
            o_ref[:, pr * LANES:(pr + 1) * LANES] = _dot(cat, bd_ref[pr]).astype(BF16)


def _mla_prompt(q, kp, bd, B, S):
    tq, tk = MLA_TQ, MLA_TK
    nq = S // tq
    pairs = [(qi, ki) for qi in range(nq) for ki in range((qi * tq + tq - 1) // tk + 1)]
    qi_tab = jnp.asarray(np.array([p[0] for p in pairs], np.int32))
    ki_tab = jnp.asarray(np.array([p[1] for p in pairs], np.int32))
    H = MLA_HEADS
    grid_spec = pltpu.PrefetchScalarGridSpec(
        num_scalar_prefetch=2,
        grid=(B, len(pairs)),
        in_specs=[pl.BlockSpec((H, tq, QK_W), lambda b, s, qt, kt: (0, b * nq + qt[s], 0)),
                  pl.BlockSpec((tk, QK_W), lambda b, s, qt, kt: (b * (S // tk) + kt[s], 0)),
                  pl.BlockSpec(bd.shape, lambda b, s, qt, kt: (0, 0, 0))],
        out_specs=pl.BlockSpec((tq, H * MLA_V), lambda b, s, qt, kt: (b * nq + qt[s], 0)),
        scratch_shapes=[pltpu.VMEM((H, tq, 1), F32), pltpu.VMEM((H, tq, 1), F32),
                        pltpu.VMEM((H, tq, MLA_KV_LORA), F32)],
    )
    return pl.pallas_call(
        functools.partial(_mla_prompt_kernel, tq=tq, tk=tk),
        out_shape=jax.ShapeDtypeStruct((B * S, H * MLA_V), BF16),
        grid_spec=grid_spec,
        compiler_params=_cp("parallel", "arbitrary"),
        name="mla_prompt",
    )(qi_tab, ki_tab, q, kp, bd)


def _sort_key(score):
    bits = lax.bitcast_convert_type(score, I32)
    key = jnp.where(bits < 0, bits ^ 0x7FFFFFFF, bits)
    return jnp.where(score == 0.0, 0, key)


def _count(keys_ref, nchunks, pred):
    rows = keys_ref.shape[0]

    def body(c, cnt):
        k = keys_ref[:, pl.ds(pl.multiple_of(c * LANES, LANES), LANES)]
        return cnt + jnp.where(pred(k), 1.0, 0.0)

    cnt = lax.fori_loop(0, nchunks, body, jnp.zeros((rows, LANES), F32))
    return jnp.sum(cnt, axis=-1, keepdims=True)


def _select_mask(keys_ref, am_ref, nchunks, n_sel):
    rows = keys_ref.shape[0]

    def bit_body(i, w):
        candw = w | lax.shift_left(jnp.int32(1), 31 - i)
        cand = jnp.broadcast_to(candw ^ INT_MIN, (rows, LANES))
        tot = _count(keys_ref, nchunks, lambda k: k >= cand)
        return jnp.where(tot >= n_sel, candw, w)

    thr = lax.fori_loop(0, 32, bit_body, jnp.zeros((rows, 1), I32)) ^ INT_MIN
    thr_b = jnp.broadcast_to(thr, (rows, LANES))
    n_gt = _count(keys_ref, nchunks, lambda k: k > thr_b)
    n_ge = _count(keys_ref, nchunks, lambda k: k >= thr_b)
    tied = jnp.max(jnp.where((n_ge > n_sel) & (thr != KEY_NEG), 1.0, 0.0)) > 0.0

    @pl.when(jnp.logical_not(tied))
    def _():
        def body(c, carry):
            sl = pl.ds(pl.multiple_of(c * LANES, LANES), LANES)
            k = keys_ref[:, sl]
            am_ref[:, sl] = jnp.where((k >= thr_b) & (k != KEY_NEG), 0.0, NEG_INF)
            return carry
        lax.fori_loop(0, nchunks, body, 0)

    @pl.when(tied)
    def _():
        need = jnp.broadcast_to(n_sel - n_gt, (rows, LANES))
        jj = lax.broadcasted_iota(I32, (LANES, LANES), 0)
        cc = lax.broadcasted_iota(I32, (LANES, LANES), 1)
        before = jnp.where(jj < cc, 1.0, 0.0).astype(BF16)
        ones = jnp.ones((LANES, LANES), BF16)

        def body(c, base):
            sl = pl.ds(pl.multiple_of(c * LANES, LANES), LANES)
            k = keys_ref[:, sl]
            eq = k == thr_b
            eqb = jnp.where(eq, 1.0, 0.0).astype(BF16)
            rank = base + _dot(eqb, before)
            sel = ((k > thr_b) | (eq & (rank < need))) & (k != KEY_NEG)
            am_ref[:, sl] = jnp.where(sel, 0.0, NEG_INF)
            return base + _dot(eqb, ones)
        lax.fori_loop(0, nchunks, body, jnp.zeros((rows, LANES), F32))


def _dsa_prompt_kernel(qc_ref, krep_ref, vrep_ref, qi_ref, kirep_ref, wi_ref, bias_ref, o_ref,
                       keys_ref, am_ref, m_ref, l_ref, acc_ref, *, tq, tk, n_sel):
    qidx = pl.program_id(1)
    nkb = (qidx * tq + tq - 1) // tk + 1
    G = DSA_GROUP
    W = G * DSA_HEAD_DIM
    lane = lax.broadcasted_iota(I32, (tq, W), 1)
    hmask = [(lane >= hh * DSA_HEAD_DIM) & (lane < (hh + 1) * DSA_HEAD_DIM) for hh in range(G)]

    def stack_heads(x):
        return jnp.concatenate([jnp.where(hmask[hh], x, jnp.zeros_like(x)) for hh in range(G)], axis=0)

    qim = stack_heads(qi_ref[...])
    wi = wi_ref[...]
    wcol = [wi[:, hh:hh + 1] for hh in range(IDX_HEADS)]
    qpos = qidx * tq + lax.broadcasted_iota(I32, (tq, tk), 0)
    kcol = lax.broadcasted_iota(I32, (tq, tk), 1)

    def p1(kb, carry):
        ks = pl.multiple_of(kb * tk, tk)
        d = jnp.maximum(_dot_nt(qim, kirep_ref[pl.ds(ks, tk), :]), 0.0)
        sc = d[0:tq] * wcol[0]
        for hh in range(1, IDX_HEADS):
            sc = sc + d[hh * tq:(hh + 1) * tq] * wcol[hh]
        key = jnp.where(kb * tk + kcol <= qpos, _sort_key(sc), KEY_NEG)
        keys_ref[:, pl.ds(ks, tk)] = key
        return carry
    lax.fori_loop(0, nkb, p1, 0)

    _select_mask(keys_ref, am_ref, nkb * (tk // LANES), n_sel)

    for g in range(DSA_KV_HEADS):
        qm = stack_heads(qc_ref[:, g * W:(g + 1) * W])
        m_ref[...] = jnp.full(m_ref.shape, NEG_INF, F32)
        l_ref[...] = jnp.zeros(l_ref.shape, F32)
        acc_ref[...] = jnp.zeros(acc_ref.shape, F32)

        def p3(kb, carry):
            ks = pl.multiple_of(kb * tk, tk)
            s = _dot_nt(qm, krep_ref[pl.ds(ks, tk), g * W:(g + 1) * W]).reshape(G, tq, tk)
            ty = jnp.minimum((qidx * tq - kb * tk) // tq, bias_ref.shape[0] - 1)
            bias = jnp.stack([bias_ref[ty, g * G + hh] for hh in range(G)], axis=0)
            s = s + bias + am_ref[:, pl.ds(ks, tk)][None]
            m_old = m_ref[...]
            m_new = jnp.maximum(m_old, jnp.max(s, axis=-1, keepdims=True))
            m_use = jnp.where(m_new == NEG_INF, 0.0, m_new)
            alpha = jnp.exp(m_old - m_use)
            p = jnp.exp(s - m_use)
            l_ref[...] = alpha * l_ref[...] + jnp.sum(p, axis=-1, keepdims=True)
            pv = _dot(p.reshape(G * tq, tk).astype(BF16), vrep_ref[pl.ds(ks, tk), g * W:(g + 1) * W])
            acc_ref[...] = alpha * acc_ref[...] + pv.reshape(G, tq, W)
            m_ref[...] = m_new
            return carry
        lax.fori_loop(0, nkb, p3, 0)

        o = acc_ref[...] / l_ref[...]
        og = jnp.where(hmask[0], o[0], 0.0)
        for hh in range(1, G):
            og = og + jnp.where(hmask[hh], o[hh], 0.0)
        o_ref[:, g * W:(g + 1) * W] = og.astype(BF16)


def _dsa_prompt(qc, krep, vrep, qi, kirep, wi, bias, B, S):
    tq, tk = DSA_TQ, DSA_TK
    nq = S // tq
    n_sel = min(DSA_TOPK_MAX, S // 4)
    G = DSA_GROUP
    W = G * DSA_HEAD_DIM
    qrow = lambda w: pl.BlockSpec((tq, w), lambda b, i: (b * nq + i, 0))
    seq = lambda w: pl.BlockSpec((S, w), lambda b, i: (b, 0))
    return pl.pallas_call(
        functools.partial(_dsa_prompt_kernel, tq=tq, tk=tk, n_sel=n_sel),
        out_shape=jax.ShapeDtypeStruct((B * S, DSA_HEADS * DSA_HEAD_DIM), BF16),
        grid=(B, nq),
        in_specs=[qrow(512), seq(512), seq(512), qrow(256), seq(256), qrow(LANES), _whole(bias.shape)],
        out_specs=qrow(512),
        scratch_shapes=[pltpu.VMEM((tq, S), I32), pltpu.VMEM((tq, S), F32),
                        pltpu.VMEM((G, tq, 1), F32), pltpu.VMEM((G, tq, 1), F32),
                        pltpu.VMEM((G, tq, W), F32)],
        compiler_params=_cp("parallel", "arbitrary"),
        name="dsa_prompt",
    )(qc, krep, vrep, qi, kirep, wi, bias)


def _lru_gates(xc, wa, ba, wg, bg, coef):
    xcb = xc.astype(BF16)
    r = jax.nn.sigmoid(_dot(xcb, wa) + ba)
    gi = jax.nn.sigmoid(_dot(xcb, wg) + bg)
    log_a = coef * r
    return jnp.exp(log_a), xc * gi, jnp.sqrt(1.0 - jnp.exp(2.0 * log_a))


def _lru_coef(lam):
    z = -lam
    return -LRU_C * (jnp.maximum(z, 0.0) + jnp.log1p(jnp.exp(-jnp.abs(z))))


def _rglru_prompt_kernel(x_ref, wc_ref, bc_ref, wa_ref, ba_ref, wg_ref, bg_ref, lam_ref, o_ref, hT_ref,
                         xpad_ref, a_ref, u_ref, *, chunk):
    S, W = x_ref.shape
    PAD = 8
    xpad_ref[0:PAD, :] = jnp.zeros((PAD, W), F32)
    coef = _lru_coef(lam_ref[...])
    wa = wa_ref[...]
    wg = wg_ref[...]
    for c in range(S // chunk):
        r0 = c * chunk
        xpad_ref[PAD + r0:PAD + r0 + chunk, :] = x_ref[r0:r0 + chunk, :]
        xc = bc_ref[...] + x_ref[r0:r0 + chunk, :] * wc_ref[CONV_W - 1:CONV_W, :]
        for k in range(CONV_W - 1):
            sh = CONV_W - 1 - k
            xc = xc + xpad_ref[PAD + r0 - sh:PAD + r0 - sh + chunk, :] * wc_ref[k:k + 1, :]
        a, xg, mult = _lru_gates(xc, wa, ba_ref[...], wg, bg_ref[...], coef)
        if c == 0:
            first = lax.broadcasted_iota(I32, (chunk, W), 0) == 0
            a = jnp.where(first, 0.0, a)
            mult = jnp.where(first, 1.0, mult)
        a_ref[r0:r0 + chunk, :] = a
        u_ref[r0:r0 + chunk, :] = xg * mult

    rowi = lax.broadcasted_iota(I32, (8, W), 0)

    def scan8(r, h):
        a8 = a_ref[pl.ds(r, 8), :]
        u8 = u_ref[pl.ds(r, 8), :]
        for s in (1, 2, 4):
            keep = rowi >= s
            u8 = jnp.where(keep, a8 * pltpu.roll(u8, s, axis=0) + u8, u8)
            a8 = jnp.where(keep, a8 * pltpu.roll(a8, s, axis=0), a8)
        return u8 + a8 * h

    def body(i, h):
        r = pl.multiple_of(i * 16, 16)
        h0 = scan8(r, h)
        h1 = scan8(r + 8, h0[7:8, :])
        o_ref[pl.ds(r, 16), :] = jnp.concatenate([h0, h1], axis=0).astype(o_ref.dtype)
        return h1[7:8, :]

    hT_ref[...] = lax.fori_loop(0, S // 16, body, jnp.zeros((1, W), F32))


def _rglru_prompt(xr, lw, B, S):
    W = LRU_WIDTH
    vec = _whole((1, W))
    return pl.pallas_call(
        functools.partial(_rglru_prompt_kernel, chunk=256),
        out_shape=[jax.ShapeDtypeStruct((B * S, W), BF16), jax.ShapeDtypeStruct((B, 1, W), F32)],
        grid=(B,),
        in_specs=[pl.BlockSpec((S, W), lambda b: (b, 0)), _whole((CONV_W, W)), vec,
                  _whole((W, W)), vec, _whole((W, W)), vec, vec],
        out_specs=[pl.BlockSpec((S, W), lambda b: (b, 0)), pl.BlockSpec((None, 1, W), lambda b: (b, 0, 0))],
        scratch_shapes=[pltpu.VMEM((S + 8, W), F32), pltpu.VMEM((S, W), F32), pltpu.VMEM((S, W), F32)],
        compiler_params=_cp("parallel"),
        name="rglru_prompt",
    )(xr, lw["w_conv"], lw["b_conv"], lw["w_rg_a"], lw["b_rg_a"], lw["w_rg_i"], lw["b_rg_i"], lw["lru_lambda"])


def _rglru_sample_kernel(x_ref, cb_ref, h0_ref, wc_ref, bc_ref, wa_ref, ba_ref, wg_ref, bg_ref, lam_ref,
                         o_ref, hT_ref):
    T = x_ref.shape[0]
    xp = [cb_ref[k] for k in range(CONV_W - 1)] + [x_ref[t] for t in range(T)]
    coef = _lru_coef(lam_ref[...])
    h = h0_ref[...]
    for t in range(T):
        xc = bc_ref[...]
        for k in range(CONV_W):
            xc = xc + xp[t + k] * wc_ref[k:k + 1, :]
        a, xg, mult = _lru_gates(xc, wa_ref[...], ba_ref[...], wg_ref[...], bg_ref[...], coef)
        h = a * h + xg * mult
        o_ref[t] = h.astype(o_ref.dtype)
    hT_ref[...] = h


def _rglru_sample(xr_t, conv_t, h0, lw):
    T, Bd, W = xr_t.shape
    return pl.pallas_call(
        _rglru_sample_kernel,
        out_shape=[jax.ShapeDtypeStruct((T, Bd, W), BF16), jax.ShapeDtypeStruct((Bd, W), F32)],
        compiler_params=pltpu.CompilerParams(vmem_limit_bytes=VMEM_LIMIT),
        name="rglru_sample",
    )(xr_t, conv_t, h0, lw["w_conv"], lw["b_conv"], lw["w_rg_a"], lw["b_rg_a"], lw["w_rg_i"], lw["b_rg_i"],
      lw["lru_lambda"])


def _merge_kernel(x_ref, om_ref, ol_ref, od_ref, g_ref, gm_ref, wm_ref, wl_ref, wd_ref, wo_ref, o_ref):
    D = x_ref.shape[1]
    y = (g_ref[:, 0:D].astype(F32) * _dot(om_ref[...], wm_ref[...])
         + g_ref[:, D:2 * D].astype(F32) * _dot(ol_ref[...], wl_ref[...])
         + g_ref[:, 2 * D:3 * D].astype(F32) * _dot(od_ref[...], wd_ref[...]))
    o_ref[...] = x_ref[...] + gm_ref[...] * _dot(y.astype(BF16), wo_ref[...])


def _merge(x, o_mla, o_lru, o_dsa, gates, gate_mix, group_rows, lw):
    N, D = x.shape
    tm = ROW_TILE
    R = gate_mix.shape[1]
    tpg = group_rows // tm
    row = lambda w: pl.BlockSpec((tm, w), lambda i: (i, 0))
    return pl.pallas_call(
        _merge_kernel,
        out_shape=jax.ShapeDtypeStruct((N, D), F32),
        grid=(N // tm,),
        in_specs=[row(D), row(512), row(512), row(512), row(N_BRANCH * D),
                  pl.BlockSpec((None, R, D), lambda i: (i // tpg, 0, 0)),
                  _whole((512, D)), _whole((512, D)), _whole((512, D)), _whole((D, D))],
        out_specs=row(D),
        compiler_params=_cp("parallel"),
        name="merge",
    )(x, o_mla, o_lru, o_dsa, gates, gate_mix, lw["w_br_mla"], lw["w_br_lru"], lw["w_br_dsa"], lw["w_o"])


def _router_kernel(x_ref, gn_ref, sh_ref, sc_ref, wr_ref, br_ref, h_out, e_out, g_out):
    tm = x_ref.shape[0]
    h = _rms(x_ref[...], gn_ref[...]) * (1.0 + sc_ref[...]) + sh_ref[...]
    hb = h.astype(BF16)
    h_out[...] = hb
    logits = _dot(hb, wr_ref[...]) + br_ref[...]
    lane = lax.broadcasted_iota(I32, (tm, LANES), 1)
    lanef = lane.astype(F32)
    vals, idxs = [], []
    for _ in range(TOP_K):
        mx = jnp.max(logits, axis=-1, keepdims=True)
        ix = jnp.min(jnp.where(logits == mx, lanef, float(LANES)), axis=-1, keepdims=True)
        vals.append(mx)
        idxs.append(ix)
        logits = jnp.where(lanef == ix, NEG_INF, logits)
    ex = [jnp.exp(v - vals[0]) for v in vals]
    den = ex[0]
    for e in ex[1:]:
        den = den + e
    earr = jnp.zeros((tm, LANES), F32)
    garr = jnp.zeros((tm, LANES), F32)
    for k in range(TOP_K):
        earr = jnp.where(lane == k, idxs[k], earr)
        garr = jnp.where(lane == k, ex[k] / den, garr)
    e_out[...] = earr[:, 0:TOP_K].astype(I32)
    g_out[...] = garr[:, 0:TOP_K]


def _router(x, shift, scale, group_rows, lw):
    N, D = x.shape
    tm = ROW_TILE
    R = shift.shape[1]
    tpg = group_rows // tm
    row = lambda w: pl.BlockSpec((tm, w), lambda i: (i, 0))
    mod = pl.BlockSpec((None, R, D), lambda i: (i // tpg, 0, 0))
    return pl.pallas_call(
        _router_kernel,
        out_shape=[jax.ShapeDtypeStruct((N, D), BF16), jax.ShapeDtypeStruct((N, TOP_K), I32),
                   jax.ShapeDtypeStruct((N, TOP_K), F32)],
        grid=(N // tm,),
        in_specs=[row(D), _whole((1, D)), mod, mod, _whole((D, LANES)), _whole((1, LANES))],
        out_specs=[row(D), row(TOP_K), row(TOP_K)],
        compiler_params=_cp("parallel"),
        name="router",
    )(x, lw["g_norm_ffn"], shift, scale, lw["w_router"], lw["b_router"])


def _expert_kernel(be_ref, x_ref, wu_ref, bu_ref, wd_ref, bd_ref, o_ref):
    F = wd_ref.shape[0]
    z = _dot(x_ref[...], wu_ref[...].astype(BF16)) + bu_ref[...]
    glu = jnp.minimum(z[:, 0:F], SWIGLU_LIMIT)
    lin = jnp.clip(z[:, F:2 * F], -SWIGLU_LIMIT, SWIGLU_LIMIT)
    act = glu * jax.nn.sigmoid(SWIGLU_ALPHA * glu) * (lin + 1.0)
    o_ref[...] = (_dot(act.astype(BF16), wd_ref[...].astype(BF16)) + bd_ref[...]).astype(o_ref.dtype)


def _experts(xs, blk_e, w_up, b_up, w_down, b_down):
    Mp, D = xs.shape
    E, _, F2 = w_up.shape
    F = F2 // 2
    bm = MOE_BLOCK
    grid_spec = pltpu.PrefetchScalarGridSpec(
        num_scalar_prefetch=1,
        grid=(Mp // bm,),
        in_specs=[pl.BlockSpec((bm, D), lambda j, be: (j, 0)),
                  pl.BlockSpec((None, D, F2), lambda j, be: (be[j], 0, 0)),
                  pl.BlockSpec((None, 1, F2), lambda j, be: (be[j], 0, 0)),
                  pl.BlockSpec((None, F, D), lambda j, be: (be[j], 0, 0)),
                  pl.BlockSpec((None, 1, D), lambda j, be: (be[j], 0, 0))],
        out_specs=pl.BlockSpec((bm, D), lambda j, be: (j, 0)),
    )
    return pl.pallas_call(
        _expert_kernel,
        out_shape=jax.ShapeDtypeStruct((Mp, D), BF16),
        grid_spec=grid_spec,
        compiler_params=_cp("arbitrary"),
        name="experts",
    )(blk_e, xs, w_up, b_up.reshape(E, 1, F2), w_down, b_down.reshape(E, 1, D))


def _combine_kernel(x_ref, y_ref, g_ref, gf_ref, gfin_ref, o_ref, *, final):
    g = g_ref[...]
    y = g[:, 0:1] * y_ref[0].astype(F32)
    for k in range(1, TOP_K):
        y = y + g[:, k:k + 1] * y_ref[k].astype(F32)
    out = x_ref[...] + gf_ref[...] * y
    if final:
        out = _rms(out, gfin_ref[...])
    o_ref[...] = out


def _combine(x, y4, gate, row0, gate_ffn, group_rows, g_final, final):
    N, D = x.shape
    tm = ROW_TILE
    R = gate_ffn.shape[1]
    tpg = group_rows // tm
    t0 = row0 // tm
    row = lambda w: pl.BlockSpec((tm, w), lambda i: (i, 0))
    return pl.pallas_call(
        functools.partial(_combine_kernel, final=final),
        out_shape=jax.ShapeDtypeStruct((N, D), F32),
        grid=(N // tm,),
        in_specs=[row(D), pl.BlockSpec((TOP_K, tm, D), lambda i: (0, i + t0, 0)),
                  pl.BlockSpec((tm, TOP_K), lambda i: (i + t0, 0)),
                  pl.BlockSpec((None, R, D), lambda i: (i // tpg, 0, 0)), _whole((1, D))],
        out_specs=row(D),
        compiler_params=_cp("parallel"),
        name="combine",
    )(x, y4, gate, gate_ffn, g_final)


def _online_update(s, v, m_ref, l_ref, acc_ref):
    m_old = m_ref[...]
    m_new = jnp.maximum(m_old, jnp.max(s, axis=-1, keepdims=True))
    m_use = jnp.where(m_new == NEG_INF, 0.0, m_new)
    alpha = jnp.exp(m_old - m_use)
    p = jnp.exp(s - m_use)
    l_ref[...] = alpha * l_ref[...] + jnp.sum(p, axis=-1, keepdims=True)
    acc_ref[...] = alpha * acc_ref[...] + _dot(p.astype(BF16), v)
    m_ref[...] = m_new


def _init_softmax(m_ref, l_ref, acc_ref):
    m_ref[...] = jnp.full(m_ref.shape, NEG_INF, F32)
    l_ref[...] = jnp.zeros(l_ref.shape, F32)
    acc_ref[...] = jnp.zeros(acc_ref.shape, F32)


def _mla_sample_kernel(qa_ref, qp_ref, lat_ref, kpe_ref, latn_ref, kpen_ref, o_ref, m_ref, l_ref, acc_ref,
                       *, T, nkt):
    j = pl.program_id(1)

    @pl.when(j == 0)
    def _():
        _init_softmax(m_ref, l_ref, acc_ref)

    def update(lat, kpe, mask):
        s = _dot_nt(qa_ref[...], lat) + _dot_nt(qp_ref[...], kpe)
        if mask is not None:
            s = jnp.where(mask, s, NEG_INF)
        _online_update(s, lat, m_ref, l_ref, acc_ref)

    @pl.when(j < nkt)
    def _():
        update(lat_ref[...].astype(BF16), kpe_ref[...].astype(BF16), None)

    @pl.when(j == nkt)
    def _():
        R, C = qa_ref.shape[0], latn_ref.shape[0]
        t = lax.broadcasted_iota(I32, (R, C), 0) % T
        c = lax.broadcasted_iota(I32, (R, C), 1)
        update(latn_ref[...].astype(BF16), kpen_ref[...].astype(BF16), c <= t)
        o_ref[...] = acc_ref[...] / l_ref[...]


def _mla_sample(qa, qp, lat, kpe, latn, kpen, T):
    Bd, R, C = qa.shape
    past = lat.shape[1]
    tk = min(SAMPLE_TK, past)
    nkt = past // tk
    NP = latn.shape[1]
    per_b = lambda r, w: pl.BlockSpec((None, r, w), lambda b, j: (b, 0, 0))
    tile = lambda w: pl.BlockSpec((None, tk, w), lambda b, j: (b, jnp.minimum(j, nkt - 1), 0))
    return pl.pallas_call(
        functools.partial(_mla_sample_kernel, T=T, nkt=nkt),
        out_shape=jax.ShapeDtypeStruct((Bd, R, C), F32),
        grid=(Bd, nkt + 1),
        in_specs=[per_b(R, C), per_b(R, MLA_ROPE), tile(C), tile(MLA_ROPE), per_b(NP, C), per_b(NP, MLA_ROPE)],
        out_specs=per_b(R, C),
        scratch_shapes=[pltpu.VMEM((R, 1), F32), pltpu.VMEM((R, 1), F32), pltpu.VMEM((R, C), F32)],
        compiler_params=_cp("parallel", "arbitrary"),
        name="mla_sample",
    )(qa, qp, lat, kpe, latn, kpen)


def _idx_sample_kernel(qi_ref, wi_ref, ki_ref, kin_ref, am_ref, keys_ref, *, T, TP, nkt, tk, n_sel):
    j = pl.program_id(1)
    wi = wi_ref[...]

    def scores(ki):
        d = jnp.maximum(_dot_nt(qi_ref[...], ki), 0.0)
        sc = d[0:TP] * wi[:, 0:1]
        for hh in range(1, IDX_HEADS):
            sc = sc + d[hh * TP:(hh + 1) * TP] * wi[:, hh:hh + 1]
        return _sort_key(sc)

    @pl.when(j < nkt)
    def _():
        key = scores(ki_ref[...].astype(BF16))
        t = lax.broadcasted_iota(I32, key.shape, 0)
        keys_ref[:, pl.ds(pl.multiple_of(j * tk, tk), tk)] = jnp.where(t < T, key, KEY_NEG)

    @pl.when(j == nkt)
    def _():
        key = scores(kin_ref[...].astype(BF16))
        t = lax.broadcasted_iota(I32, key.shape, 0)
        c = lax.broadcasted_iota(I32, key.shape, 1)
        keys_ref[:, nkt * tk:nkt * tk + key.shape[1]] = jnp.where((t < T) & (c <= t), key, KEY_NEG)
        _select_mask(keys_ref, am_ref, keys_ref.shape[1] // LANES, n_sel)


def _idx_sample(qi, wi, ki, kin, T, n_sel):
    Bd, R, _ = qi.shape
    TP = wi.shape[1]
    past = ki.shape[1]
    tk = min(SAMPLE_TK, past)
    nkt = past // tk
    NP = kin.shape[1]
    per_b = lambda r, w: pl.BlockSpec((None, r, w), lambda b, j: (b, 0, 0))
    return pl.pallas_call(
        functools.partial(_idx_sample_kernel, T=T, TP=TP, nkt=nkt, tk=tk, n_sel=n_sel),
        out_shape=jax.ShapeDtypeStruct((Bd, TP, past + NP), F32),
        grid=(Bd, nkt + 1),
        in_specs=[per_b(R, IDX_DIM), per_b(TP, LANES),
                  pl.BlockSpec((None, tk, IDX_DIM), lambda b, j: (b, jnp.minimum(j, nkt - 1), 0)),
                  per_b(NP, IDX_DIM)],
        out_specs=per_b(TP, past + NP),
        scratch_shapes=[pltpu.VMEM((TP, past + NP), I32)],
        compiler_params=_cp("parallel", "arbitrary"),
        name="idx_sample",
    )(qi, wi, ki, kin)


def _dsa_sample_kernel(q_ref, k_ref, v_ref, kn_ref, vn_ref, am_ref, amn_ref, b_ref, bn_ref, o_ref,
                       m_ref, l_ref, acc_ref, *, T, nkt):
    j = pl.program_id(1)
    H = DSA_HEADS

    @pl.when(j == 0)
    def _():
        _init_softmax(m_ref, l_ref, acc_ref)

    def update(k, v, am, bias):
        w = am.shape[1]
        amr = jnp.concatenate([jnp.broadcast_to(am[t:t + 1, :], (H, w)) for t in range(T)], axis=0)
        s = _dot_nt(q_ref[...], k) + bias + amr
        _online_update(s, v, m_ref, l_ref, acc_ref)

    @pl.when(j < nkt)
    def _():
        update(k_ref[...].astype(BF16), v_ref[...].astype(BF16), am_ref[...], b_ref[...])

    @pl.when(j == nkt)
    def _():
        update(kn_ref[...].astype(BF16), vn_ref[...].astype(BF16), amn_ref[...], bn_ref[...])
        o_ref[...] = acc_ref[...] / l_ref[...]


def _dsa_sample(q, k, v, kn, vn, am, bias, T):
    Bd, R, C = q.shape
    past = k.shape[1]
    tk = min(SAMPLE_TK, past)
    nkt = past // tk
    NP = kn.shape[1]
    TP = am.shape[1]
    per_b = lambda r, w: pl.BlockSpec((None, r, w), lambda b, j: (b, 0, 0))
    tile = lambda w: pl.BlockSpec((None, tk, w), lambda b, j: (b, jnp.minimum(j, nkt - 1), 0))
    return pl.pallas_call(
        functools.partial(_dsa_sample_kernel, T=T, nkt=nkt),
        out_shape=jax.ShapeDtypeStruct((Bd, R, C), F32),
        grid=(Bd, nkt + 1),
        in_specs=[per_b(R, C), tile(C), tile(C), per_b(NP, C), per_b(NP, C),
                  pl.BlockSpec((None, TP, tk), lambda b, j: (b, 0, jnp.minimum(j, nkt - 1))),
                  pl.BlockSpec((None, TP, NP), lambda b, j: (b, 0, past // NP)),
                  pl.BlockSpec((R, tk), lambda b, j: (0, jnp.minimum(j, nkt - 1))),
                  pl.BlockSpec((R, NP), lambda b, j: (0, past // NP))],
        out_specs=per_b(R, C),
        scratch_shapes=[pltpu.VMEM((R, 1), F32), pltpu.VMEM((R, 1), F32), pltpu.VMEM((R, C), F32)],
        compiler_params=_cp("parallel", "arbitrary"),
        name="dsa_sample",
    )(q, k, v, kn, vn, am, am, bias, bias)


def _rope_tables(pos):
    half = MLA_ROPE // 2
    freq = ROPE_THETA ** (-np.arange(half, dtype=np.float64) / half)
    ang = np.asarray(pos, np.float64)[:, None] * freq[None, :]
    cos = np.concatenate([np.cos(ang), np.cos(ang)], axis=1)
    sin = np.concatenate([-np.sin(ang), np.sin(ang)], axis=1)
    reps = 2 * LANES // MLA_ROPE
    return (jnp.asarray(np.tile(cos, (1, reps)).astype(np.float32)),
            jnp.asarray(np.tile(sin, (1, reps)).astype(np.float32)))


def _block_diag(w):
    n, c, d = w.shape
    eye = jnp.eye(n, dtype=w.dtype)
    return (eye[:, None, :, None] * w[:, :, None, :]).reshape(n * c, n * d)


def _prep_layer(l, p):
    D = p["w_in"].shape[1]
    w = p["w_in"][l]
    sizes = (MLA_Q_LORA, MLA_KV_LORA, MLA_ROPE, LRU_WIDTH, DSA_HEADS * DSA_HEAD_DIM,
             DSA_KV_HEADS * DSA_HEAD_DIM, DSA_KV_HEADS * DSA_HEAD_DIM, IDX_HEADS * IDX_DIM, IDX_DIM,
             IDX_HEADS, N_BRANCH * D)
    offs = np.cumsum(sizes)[:-1].tolist()
    q_lat, kv_lat, kpe, xr, q_c, k_c, v_c, q_i, k_i, w_i, g = jnp.split(w, offs, axis=1)
    half = MLA_ROPE // 2
    kpe_sw = jnp.concatenate([kpe[:, half:], kpe[:, :half]], axis=1)
    padto = lambda a: jnp.pad(a, ((0, 0), (0, LANES - a.shape[1])))
    w_in = jnp.concatenate([q_lat, kv_lat, jnp.tile(kpe, (1, ROPE_SLOT)), jnp.tile(kpe_sw, (1, ROPE_SLOT)),
                            xr, q_c, k_c, v_c, q_i, padto(k_i), padto(w_i), g], axis=1).astype(BF16)
    assert w_in.shape[1] == _C_G + N_BRANCH * D
    w_uq = p["w_uq"][l]
    rope = w_uq[:, :, MLA_NOPE:]
    rope_sw = jnp.concatenate([rope[:, :, half:], rope[:, :, :half]], axis=2)
    w_uv = p["w_uv"][l]
    pairs = [_block_diag(jnp.stack([w_uv[:, 2 * i], w_uv[:, 2 * i + 1]])) for i in range(MLA_HEADS // 2)]
    row = lambda a: a.reshape(1, -1)
    E = p["w_router"].shape[2]
    return {
        "w_in": w_in,
        "g_norm_mix": row(p["g_norm_mix"][l]), "g_norm_ffn": row(p["g_norm_ffn"][l]),
        "g_q_norm": row(p["g_q_norm"][l]), "g_kv_norm": row(p["g_kv_norm"][l]),
        "wn": jnp.transpose(w_uq[:, :, :MLA_NOPE], (1, 0, 2)).astype(BF16),
        "wr": rope.reshape(MLA_Q_LORA, -1).astype(BF16), "wrs": rope_sw.reshape(MLA_Q_LORA, -1).astype(BF16),
        "wuk": jnp.transpose(p["w_uk"][l], (1, 2, 0)).astype(BF16),
        "bd_pair": jnp.stack(pairs).astype(BF16),
        "bd_full": _block_diag(jnp.transpose(w_uv, (1, 0, 2))).astype(BF16),
        "w_conv": p["w_conv"][l], "b_conv": row(p["b_conv"][l]),
        "w_rg_a": _block_diag(p["w_rg_a"][l]).astype(BF16), "b_rg_a": row(p["b_rg_a"][l]),
        "w_rg_i": _block_diag(p["w_rg_i"][l]).astype(BF16), "b_rg_i": row(p["b_rg_i"][l]),
        "lru_lambda": row(p["lru_lambda"][l]),
        "w_br_mla": p["w_br_mla"][l].astype(BF16), "w_br_lru": p["w_br_lru"][l].astype(BF16),
        "w_br_dsa": p["w_br_dsa"][l].astype(BF16), "w_o": p["w_o"][l].astype(BF16),
        "w_router": jnp.pad(p["w_router"][l], ((0, 0), (0, LANES - E))).astype(BF16),
        "b_router": jnp.pad(row(p["b_router"][l]), ((0, 0), (0, LANES - E)), constant_values=NEG_INF),
    }


def _moe(h_all, eidx, w_up, b_up, w_down, b_down):
    N, D = h_all.shape
    E = w_up.shape[0]
    bm = MOE_BLOCK
    M = N * TOP_K
    flat_e = eidx.reshape(M)
    order = jnp.argsort(flat_e)
    e_sorted = flat_e[order]
    counts = jnp.bincount(flat_e, length=E)
    padded = (counts + bm - 1) // bm * bm
    pad_end = jnp.cumsum(padded)
    start = jnp.cumsum(counts) - counts
    dest = (pad_end - padded)[e_sorted] + jnp.arange(M) - start[e_sorted]
    n_blocks = M // bm + E
    blk_e = jnp.minimum(jnp.searchsorted(pad_end, jnp.arange(n_blocks) * bm, side="right"), E - 1).astype(I32)
    tok = jnp.zeros((n_blocks * bm,), I32).at[dest].set((order // TOP_K).astype(I32))
    pos = jnp.zeros((M,), I32).at[order].set(dest.astype(I32))
    xs = h_all[tok]
    out = _experts(xs, blk_e, w_up, b_up, w_down, b_down)
    return out[pos.reshape(N, TOP_K).T]


def kernel(x_prompt, x_sample, cache_mla_latent, cache_mla_kpe, cache_dsa_k, cache_dsa_v, cache_idx_k, state_lru_h, state_conv, page_table, c_prompt, c_sample, rel_bias, g_norm_mix, g_norm_ffn, w_ada, b_ada, w_in, g_q_norm, w_uq, g_kv_norm, w_uk, w_uv, w_conv, b_conv, w_rg_a, b_rg_a, w_rg_i, b_rg_i, lru_lambda, w_br_mla, w_br_lru, w_br_dsa, w_o, w_router, b_router, w_up, b_up, w_down, b_down, g_final):
    params = dict(g_norm_mix=g_norm_mix, g_norm_ffn=g_norm_ffn, w_in=w_in, g_q_norm=g_q_norm, w_uq=w_uq,
                  g_kv_norm=g_kv_norm, w_uk=w_uk, w_uv=w_uv, w_conv=w_conv, b_conv=b_conv, w_rg_a=w_rg_a,
                  b_rg_a=b_rg_a, w_rg_i=w_rg_i, b_rg_i=b_rg_i, lru_lambda=lru_lambda, w_br_mla=w_br_mla,
                  w_br_lru=w_br_lru, w_br_dsa=w_br_dsa, w_o=w_o, w_router=w_router, b_router=b_router)
    B, S, D = x_prompt.shape
    Bd, T, _ = x_sample.shape
    depth = w_in.shape[0]
    n_pages, page = page_table.shape[1], cache_mla_latent.shape[2]
    past = n_pages * page
    Np, Ns = B * S, Bd * T
    tm = ROW_TILE
    TP = 8
    NP = LANES
    assert S % max(tm, MLA_TQ, DSA_TK) == 0 and Ns % tm == 0 and T <= TP and past % LANES == 0

    cos_p, sin_p = _rope_tables(np.arange(S))
    cos_s, sin_s = _rope_tables(np.tile(past + np.arange(T), tm // T))
    ntypes = -(-(_BUCKET_FAR + DSA_TK - 1) // DSA_TQ) + 1
    bias_p = _bias_tiles(rel_bias, ntypes, DSA_TQ, DSA_TK, 0, DSA_TQ, 1, "bias_prompt")
    bias_s = _bias_tiles(rel_bias, T, 8, past + NP, past, 1, 0, "bias_sample")
    bias_s = bias_s[:, :, 0, :].reshape(T * DSA_HEADS, past + NP)
    n_sel_s = min(DSA_TOPK_MAX, (past + T) // 4)

    xp = x_prompt.reshape(Np, D)
    xs = x_sample.reshape(Ns, D)
    c_all = jnp.concatenate([c_prompt, c_sample], axis=0)
    new_p = [[] for _ in range(7)]
    new_s = [[] for _ in range(7)]
    per_tok = lambda a: jnp.repeat(a, T, axis=0).reshape(Ns // tm, tm, D)
    for l in range(depth):
        lw = _prep_layer(l, params)
        ada = _matmul(c_all, w_ada[l], b_ada[l].reshape(1, -1), D, "ada")
        ada_p = [a.reshape(B, 1, D) for a in jnp.split(ada[:B], 6, axis=1)]
        ada_s = [per_tok(a) for a in jnp.split(ada[B:], 6, axis=1)]

        (q, ckv, kp, kpe, xr, qc, kc, vc, krep, vrep, qi, ki, kirep, wi, gates) = _in_proj(
            xp, ada_p[0], ada_p[1], S, cos_p, sin_p, lw)
        o_mla = _mla_prompt(q, kp, lw["bd_pair"], B, S)
        o_lru, hT = _rglru_prompt(xr, lw, B, S)
        o_dsa = _dsa_prompt(qc, krep, vrep, qi, kirep, wi, bias_p, B, S)
        xp = _merge(xp, o_mla, o_lru, o_dsa, gates, ada_p[2], S, lw)
        xr3 = xr.reshape(B, S, LRU_WIDTH)
        conv_p = jnp.concatenate([jnp.zeros((B, CONV_W - 1, LRU_WIDTH), F32), xr3], axis=1)[:, S:]
        for i, a in enumerate((ckv.reshape(B, S, -1), kpe[:, :MLA_ROPE].reshape(B, S, -1),
                               kc.reshape(B, S, DSA_KV_HEADS, DSA_HEAD_DIM),
                               vc.reshape(B, S, DSA_KV_HEADS, DSA_HEAD_DIM),
                               ki[:, :IDX_DIM].reshape(B, S, -1), hT.reshape(B, -1), conv_p)):
            new_p[i].append(a)

        (q, ckv, kp, kpe, xr, qc, kc, vc, krep, vrep, qi, ki, kirep, wi, gates) = _in_proj(
            xs, ada_s[0], ada_s[1], tm, cos_s, sin_s, lw)
        H = MLA_HEADS
        q4 = jnp.transpose(q.reshape(H, Bd, T, QK_W), (1, 0, 2, 3))
        qa = q4[..., :MLA_KV_LORA].reshape(Bd, H * T, MLA_KV_LORA)
        qp = jnp.stack([q4[:, h, :, MLA_KV_LORA + (h % ROPE_SLOT) * MLA_ROPE:
                           MLA_KV_LORA + (h % ROPE_SLOT + 1) * MLA_ROPE] for h in range(H)], axis=1)
        qp = qp.reshape(Bd, H * T, MLA_ROPE)
        padk = lambda a: jnp.pad(a.reshape(Bd, T, -1), ((0, 0), (0, NP - T), (0, 0)))
        gather = lambda c: c[l][page_table].reshape(Bd, past, -1)
        o_lat = _mla_sample(qa, qp, gather(cache_mla_latent), gather(cache_mla_kpe),
                            padk(ckv), padk(kpe[:, :MLA_ROPE]), T)
        o_lat = jnp.transpose(o_lat.reshape(Bd, H, T, MLA_KV_LORA), (0, 2, 1, 3)).reshape(Ns, H * MLA_KV_LORA)
        o_mla = _matmul(o_lat, lw["bd_full"], jnp.zeros((1, H * MLA_V), F32), H * MLA_V,
                        "mla_sample_out").astype(BF16)

        xr_t = jnp.transpose(xr.reshape(Bd, T, LRU_WIDTH), (1, 0, 2))
        o_lru_t, hT = _rglru_sample(xr_t, jnp.transpose(state_conv[l], (1, 0, 2)), state_lru_h[l], lw)
        o_lru = jnp.transpose(o_lru_t, (1, 0, 2)).reshape(Ns, LRU_WIDTH)
        conv_s = jnp.concatenate([state_conv[l], xr.reshape(Bd, T, LRU_WIDTH)], axis=1)[:, T:]

        qi_s = jnp.transpose(qi.reshape(Bd, T, IDX_HEADS, IDX_DIM), (0, 2, 1, 3))
        qi_s = jnp.pad(qi_s, ((0, 0), (0, 0), (0, TP - T), (0, 0))).reshape(Bd, IDX_HEADS * TP, IDX_DIM)
        wi_s = jnp.pad(wi.reshape(Bd, T, LANES), ((0, 0), (0, TP - T), (0, 0)))
        am = _idx_sample(qi_s, wi_s, gather(cache_idx_k), padk(ki[:, :IDX_DIM]), T, n_sel_s)
        qc4 = qc.reshape(Bd, T, DSA_KV_HEADS, DSA_GROUP, DSA_HEAD_DIM)
        zeros = jnp.zeros_like(qc4)
        q_s = jnp.stack([jnp.concatenate([qc4[:, :, 0], zeros[:, :, 0]], axis=-1),
                         jnp.concatenate([zeros[:, :, 1], qc4[:, :, 1]], axis=-1)], axis=2)
        q_s = q_s.reshape(Bd, T * DSA_HEADS, DSA_KV_HEADS * DSA_HEAD_DIM)
        o_s = _dsa_sample(q_s, gather(cache_dsa_k), gather(cache_dsa_v), padk(kc), padk(vc), am, bias_s, T)
        o_s = o_s.reshape(Bd, T, DSA_KV_HEADS, DSA_GROUP, DSA_KV_HEADS, DSA_HEAD_DIM)
        o_dsa = jnp.stack([o_s[:, :, g, :, g] for g in range(DSA_KV_HEADS)], axis=2)
        o_dsa = o_dsa.reshape(Ns, DSA_HEADS * DSA_HEAD_DIM).astype(BF16)
        xs = _merge(xs, o_mla, o_lru, o_dsa, gates, ada_s[2], tm, lw)
        for i, a in enumerate((ckv.reshape(Bd, T, -1), kpe[:, :MLA_ROPE].reshape(Bd, T, -1),
                               kc.reshape(Bd, T, DSA_KV_HEADS, DSA_HEAD_DIM),
                               vc.reshape(Bd, T, DSA_KV_HEADS, DSA_HEAD_DIM),
                               ki[:, :IDX_DIM].reshape(Bd, T, -1), hT, conv_s)):
            new_s[i].append(a)

        h_p, e_p, g_p = _router(xp, ada_p[3], ada_p[4], S, lw)
        h_s, e_s, g_s = _router(xs, ada_s[3], ada_s[4], tm, lw)
        gate = jnp.concatenate([g_p, g_s], axis=0)
        y4 = _moe(jnp.concatenate([h_p, h_s], axis=0), jnp.concatenate([e_p, e_s], axis=0),
                  w_up[l], b_up[l], w_down[l], b_down[l])
        final = l == depth - 1
        gfin = g_final.reshape(1, D)
        xp = _combine(xp, y4, gate, 0, ada_p[5], S, gfin, final)
        xs = _combine(xs, y4, gate, Np, ada_s[5], tm, gfin, final)

    outs = [xp.reshape(B, S, D), xs.reshape(Bd, T, D)]
    outs += [jnp.stack(a) for a in new_p] + [jnp.stack(a) for a in new_s]
    return tuple(outs)
```

```python
import functools
import math

import numpy as np
import jax
import jax.numpy as jnp
from jax import lax
from jax.experimental import pallas as pl
from jax.experimental.pallas import tpu as pltpu

F32 = jnp.float32
BF16 = jnp.bfloat16
I32 = jnp.int32

EPS = 1e-6
MLA_HEADS = 8
MLA_NOPE = 64
MLA_ROPE = 32
MLA_V = 64
MLA_Q_LORA = 384
MLA_KV_LORA = 256
MLA_SCALE = (MLA_NOPE + MLA_ROPE) ** -0.5
ROPE_THETA = 10000.0
LRU_WIDTH = 512
LRU_BLOCKS = 8
CONV_W = 4
LRU_C = 8.0
DSA_HEADS = 8
DSA_KV_HEADS = 2
DSA_GROUP = DSA_HEADS // DSA_KV_HEADS
DSA_HEAD_DIM = 64
DSA_SCALE = DSA_HEAD_DIM ** -0.5
IDX_HEADS = 4
IDX_DIM = 64
IDX_SCALE = IDX_DIM ** -0.5
DSA_TOPK_MAX = 256
N_BUCKETS = 32
MAX_DISTANCE = 128
N_EXPERTS = 32
TOP_K = 4
SWIGLU_ALPHA = 1.702
SWIGLU_LIMIT = 7.0
N_BRANCH = 3

LANES = 128
SUBLANES = 8
ROPE_SLOT = LANES // MLA_ROPE
QK_W = MLA_KV_LORA + LANES
VMEM_LIMIT = 56 * 1024 * 1024
INT_MIN = -2 ** 31
KEY_NEG = int(np.array(-np.inf, np.float32).view(np.int32)) ^ 0x7FFFFFFF
NEG_INF = float("-inf")

ROW_TILE = 256
ATT_TQ = LANES
ATT_TK = 256
MOE_BLOCK = 256
PAGES_PER_STEP = 16
COUNT_CHAINS = 4


def _cp(*sem):
    return pltpu.CompilerParams(dimension_semantics=sem, vmem_limit_bytes=VMEM_LIMIT)


def _whole(shape):
    nd = len(shape)
    return pl.BlockSpec(shape, lambda *a: (0,) * nd)


def _dot(a, b):
    return jnp.dot(a, b, preferred_element_type=F32)


def _dot_nt(a, b):
    return lax.dot_general(a, b, (((1,), (1,)), ((), ())), preferred_element_type=F32)


def _rms(x, g):
    return x * lax.rsqrt(jnp.mean(x * x, axis=-1, keepdims=True) + EPS) * g


def _rms_t(xT, gT):
    return xT * lax.rsqrt(jnp.mean(xT * xT, axis=0, keepdims=True) + EPS) * gT


def _lane_tile(x, n):
    return x if n == 1 else jnp.concatenate([x] * n, axis=1)


def _mm_kernel(x_ref, w_ref, b_ref, o_ref):
    o_ref[...] = _dot(x_ref[...].astype(BF16), w_ref[...].astype(BF16)) + b_ref[...]


def _matmul(x, w, b, tn, name):
    M, K = x.shape
    N = w.shape[1]
    return pl.pallas_call(
        _mm_kernel,
        out_shape=jax.ShapeDtypeStruct((M, N), F32),
        grid=(N // tn,),
        in_specs=[pl.BlockSpec((M, K), lambda j: (0, 0)),
                  pl.BlockSpec((K, tn), lambda j: (0, j)),
                  pl.BlockSpec((1, tn), lambda j: (0, j))],
        out_specs=pl.BlockSpec((M, tn), lambda j: (0, j)),
        compiler_params=_cp("parallel"),
        name=name,
    )(x, w, b)


def _bucket_starts():
    n = np.arange(0, 8 * MAX_DISTANCE)
    max_exact = N_BUCKETS // 2

    def buckets(dt):
        nf = np.maximum(n, 1).astype(dt)
        large = max_exact + (np.log(nf / dt(max_exact)) / dt(math.log(MAX_DISTANCE / max_exact))
                             * dt(N_BUCKETS - max_exact)).astype(np.int32)
        return np.where(n < max_exact, n, np.minimum(large, N_BUCKETS - 1))

    b64 = buckets(np.float64)
    assert (buckets(np.float32) == b64).all() and (np.diff(b64) >= 0).all()
    starts = [int(np.argmax(b64 >= k)) if (b64 >= k).any() else int(n[-1]) + 1 for k in range(N_BUCKETS)]
    far = int(np.argmax(b64 == N_BUCKETS - 1))
    assert (b64[far:] == N_BUCKETS - 1).all()
    return starts, far


_BUCKET_STARTS, _BUCKET_FAR = _bucket_starts()


def _bias_kernel(tab_ref, o_ref, *, delta0, delta_step, row_mult, col_mult):
    i = pl.program_id(0)
    R, C = o_ref.shape[2:]
    r = lax.broadcasted_iota(I32, (R, C), 0)
    c = lax.broadcasted_iota(I32, (R, C), 1)
    dist = delta0 + delta_step * i + row_mult * r + col_mult * c
    for h in range(DSA_HEADS):
        val = jnp.full((R, C), tab_ref[0, h], F32)
        for k in range(1, N_BUCKETS):
            val = jnp.where(dist >= _BUCKET_STARTS[k], tab_ref[k, h], val)
        o_ref[0, h] = val


def _bias_tiles(rel_bias, n, R, C, delta0, delta_step, row_mult, col_mult, name):
    return pl.pallas_call(
        functools.partial(_bias_kernel, delta0=delta0, delta_step=delta_step, row_mult=row_mult,
                          col_mult=col_mult),
        out_shape=jax.ShapeDtypeStruct((n, DSA_HEADS, R, C), F32),
        grid=(n,),
        in_specs=[pl.BlockSpec(memory_space=pltpu.SMEM)],
        out_specs=pl.BlockSpec((1, DSA_HEADS, R, C), lambda i: (i, 0, 0, 0)),
        compiler_params=_cp("parallel"),
        name=name,
    )(rel_bias)


_C_KV = 0
_C_KPE = _C_KV + MLA_KV_LORA
_C_KPES = _C_KPE + LANES
_C_XR = _C_KPES + LANES
_C_KC = _C_XR + LRU_WIDTH
_C_VC = _C_KC + LANES
_C_KI = _C_VC + LANES
_C_G = _C_KI + LANES
_R_QLAT = 0
_R_KV = _R_QLAT + MLA_Q_LORA
_R_QC = _R_KV + MLA_KV_LORA
_R_QI = _R_QC + DSA_HEADS * DSA_HEAD_DIM
_R_VC = _R_QI + IDX_HEADS * IDX_DIM
_R_WI = _R_VC + LANES
_R_END = _R_WI + SUBLANES


def _inproj_kernel(x_ref, gn_ref, sh_ref, sc_ref, cos_ref, sin_ref, cosT_ref, sinT_ref, w_ref, wT_ref,
                   gqT_ref, wnT_ref, wukT_ref, wrT_ref, wrsT_ref, gkv_ref, gkvT_ref,
                   qT_out, ckv_out, kp_out, ckvT_out, kpe_out, xr_out, qcT_out, kc_out, vc_out, kcb_out,
                   vT_out, qiT_out, ki_out, kib_out, wiT_out, gates_out):
    tm, D = x_ref.shape
    nl = tm // LANES
    hb = (_rms(x_ref[...], gn_ref[...]) * (1.0 + sc_ref[...]) + sh_ref[...]).astype(BF16)

    def seg(lo, width):
        return _dot(hb, w_ref[:, lo:lo + width])

    def seg_t(lo, rows):
        return _dot_nt(wT_ref[lo:lo + rows, :], hb)

    cqT = _rms_t(seg_t(_R_QLAT, MLA_Q_LORA), _lane_tile(gqT_ref[...], nl)).astype(BF16)
    qpeT = (_dot(wrT_ref[...], cqT) * cosT_ref[...] + _dot(wrsT_ref[...], cqT) * sinT_ref[...]) * MLA_SCALE
    sub = lax.broadcasted_iota(I32, (LANES, tm), 0)
    for h in range(MLA_HEADS):
        qnT = _dot(wnT_ref[h], cqT).astype(BF16)
        qaT = (_dot(wukT_ref[h], qnT) * MLA_SCALE).astype(BF16)
        blk = qpeT[(h // ROPE_SLOT) * LANES:(h // ROPE_SLOT + 1) * LANES, :]
        slot = h % ROPE_SLOT
        keep = (sub >= slot * MLA_ROPE) & (sub < (slot + 1) * MLA_ROPE)
        peT = jnp.where(keep, blk, 0.0).astype(BF16)
        for t in range(nl):
            c0 = (t * MLA_HEADS + h) * LANES
            qT_out[0:MLA_KV_LORA, c0:c0 + LANES] = qaT[:, t * LANES:(t + 1) * LANES]
            qT_out[MLA_KV_LORA:QK_W, c0:c0 + LANES] = peT[:, t * LANES:(t + 1) * LANES]

    ckv = _rms(seg(_C_KV, MLA_KV_LORA), gkv_ref[...])
    ckv_out[...] = ckv
    cos = cos_ref[...]
    sin = sin_ref[...]
    kpe = seg(_C_KPE, LANES) * cos + seg(_C_KPES, LANES) * sin
    kpe_out[...] = kpe
    kp_out[:, 0:MLA_KV_LORA] = ckv.astype(BF16)
    kp_out[:, MLA_KV_LORA:QK_W] = kpe.astype(BF16)
    ckvT_out[...] = _rms_t(seg_t(_R_KV, MLA_KV_LORA), _lane_tile(gkvT_ref[...], nl)).astype(BF16)

    xr_out[...] = seg(_C_XR, LRU_WIDTH)

    qcT_out[...] = (seg_t(_R_QC, DSA_HEADS * DSA_HEAD_DIM) * DSA_SCALE).astype(BF16)
    kc = seg(_C_KC, LANES)
    kc_out[...] = kc
    kcb_out[...] = kc.astype(BF16)
    vc_out[...] = seg(_C_VC, LANES)
    vT_out[...] = seg_t(_R_VC, LANES).astype(BF16)
    qiT_out[...] = (seg_t(_R_QI, IDX_HEADS * IDX_DIM) * IDX_SCALE).astype(BF16)
    ki = seg(_C_KI, LANES)
    ki_out[...] = ki
    kib_out[...] = ki.astype(BF16)
    wiT_out[...] = seg_t(_R_WI, SUBLANES) * (IDX_HEADS ** -0.5)

    for j in range(N_BRANCH):
        gates_out[:, j * D:(j + 1) * D] = jax.nn.sigmoid(seg(_C_G + j * D, D)).astype(BF16)


def _in_proj(x, shift, scale, group_rows, rope, lw):
    N, D = x.shape
    tm = ROW_TILE
    R = shift.shape[1]
    tiles_per_group = group_rows // tm
    cos, sin, cosT, sinT = rope
    ptiles = cos.shape[0] // tm
    row = lambda w: pl.BlockSpec((tm, w), lambda i: (i, 0))
    col = lambda r: pl.BlockSpec((r, tm), lambda i: (0, i))
    mod = pl.BlockSpec((None, R, D), lambda i: (i // tiles_per_group, 0, 0))
    outs = [
        (jax.ShapeDtypeStruct((QK_W, N * MLA_HEADS), BF16),
         pl.BlockSpec((QK_W, tm * MLA_HEADS), lambda i: (0, i))),
        (jax.ShapeDtypeStruct((N, MLA_KV_LORA), F32), row(MLA_KV_LORA)),
        (jax.ShapeDtypeStruct((N, QK_W), BF16), row(QK_W)),
        (jax.ShapeDtypeStruct((MLA_KV_LORA, N), BF16), col(MLA_KV_LORA)),
        (jax.ShapeDtypeStruct((N, LANES), F32), row(LANES)),
        (jax.ShapeDtypeStruct((N, LRU_WIDTH), F32), row(LRU_WIDTH)),
        (jax.ShapeDtypeStruct((512, N), BF16), col(512)),
        (jax.ShapeDtypeStruct((N, LANES), F32), row(LANES)),
        (jax.ShapeDtypeStruct((N, LANES), F32), row(LANES)),
        (jax.ShapeDtypeStruct((N, LANES), BF16), row(LANES)),
        (jax.ShapeDtypeStruct((LANES, N), BF16), col(LANES)),
        (jax.ShapeDtypeStruct((256, N), BF16), col(256)),
        (jax.ShapeDtypeStruct((N, LANES), F32), row(LANES)),
        (jax.ShapeDtypeStruct((N, LANES), BF16), row(LANES)),
        (jax.ShapeDtypeStruct((SUBLANES, N), F32), col(SUBLANES)),
        (jax.ShapeDtypeStruct((N, N_BRANCH * D), BF16), row(N_BRANCH * D)),
    ]
    return pl.pallas_call(
        _inproj_kernel,
        out_shape=[o[0] for o in outs],
        grid=(N // tm,),
        in_specs=[row(D), _whole((1, D)), mod, mod,
                  pl.BlockSpec((tm, LANES), lambda i: (i % ptiles, 0)),
                  pl.BlockSpec((tm, LANES), lambda i: (i % ptiles, 0)),
                  pl.BlockSpec((2 * LANES, tm), lambda i: (0, i % ptiles)),
                  pl.BlockSpec((2 * LANES, tm), lambda i: (0, i % ptiles)),
                  _whole(lw["w_in"].shape), _whole(lw["w_inT"].shape), _whole(lw["gqT"].shape),
                  _whole(lw["wnT"].shape), _whole(lw["wukT"].shape), _whole(lw["wrT"].shape),
                  _whole(lw["wrsT"].shape), _whole((1, MLA_KV_LORA)), _whole(lw["gkvT"].shape)],
        out_specs=[o[1] for o in outs],
        compiler_params=_cp("parallel"),
        name="in_proj",
    )(x, lw["g_norm_mix"], shift, scale, cos, sin, cosT, sinT, lw["w_in"], lw["w_inT"], lw["gqT"],
      lw["wnT"], lw["wukT"], lw["wrT"], lw["wrsT"], lw["g_kv_norm"], lw["gkvT"])


def _mla_prompt_kernel(qi_tab, ki_tab, q_ref, k_ref, vT_ref, wuvT_ref, o_ref, m_ref, l_ref, acc_ref,
                       *, tq, tk):
    step = pl.program_id(1)
    qi = qi_tab[step]
    ki = ki_tab[step]
    last = (qi * tq + tq - 1) // tk
    H = MLA_HEADS

    @pl.when(ki == 0)
    def _():
        m_ref[...] = jnp.full(m_ref.shape, NEG_INF, F32)
        l_ref[...] = jnp.zeros(l_ref.shape, F32)
        acc_ref[...] = jnp.zeros(acc_ref.shape, F32)

    def update(masked):
        s = _dot(k_ref[...], q_ref[...])
        if masked:
            kpos = ki * tk + lax.broadcasted_iota(I32, (tk, H * tq), 0)
            qpos = qi * tq + (lax.broadcasted_iota(I32, (tk, H * tq), 1) & (tq - 1))
            s = jnp.where(kpos <= qpos, s, NEG_INF)
        m_old = m_ref[...]
        m_new = jnp.maximum(m_old, jnp.max(s, axis=0, keepdims=True))
        alpha = jnp.exp(m_old - m_new)
        p = jnp.exp(s - m_new)
        l_ref[...] = alpha * l_ref[...] + jnp.sum(p, axis=0, keepdims=True)
        acc_ref[...] = alpha * acc_ref[...] + _dot(vT_ref[...], p.astype(BF16))
        m_ref[...] = m_new

    @pl.when(ki < last)
    def _():
        update(False)

    @pl.when(ki == last)
    def _():
        update(True)
        oT = (acc_ref[...] / l_ref[...]).astype(BF16)
        heads = [_dot(wuvT_ref[h], oT[:, h * tq:(h + 1) * tq]) for h in range(H)]
        o_ref[...] = jnp.concatenate(heads, axis=0).T.astype(o_ref.dtype)


def _mla_prompt(qT, kp, ckvT, wuvT, B, S):
    tq, tk = ATT_TQ, ATT_TK
    nq = S // tq
    pairs = [(qi, ki) for qi in range(nq) for ki in range((qi * tq + tq - 1) // tk + 1)]
    qi_tab = jnp.asarray(np.array([p[0] for p in pairs], np.int32))
    ki_tab = jnp.asarray(np.array([p[1] for p in pairs], np.int32))
    H = MLA_HEADS
    grid_spec = pltpu.PrefetchScalarGridSpec(
        num_scalar_prefetch=2,
        grid=(B, len(pairs)),
        in_specs=[pl.BlockSpec((QK_W, H * tq), lambda b, s, qt, kt: (0, b * nq + qt[s])),
                  pl.BlockSpec((tk, QK_W), lambda b, s, qt, kt: (b * (S // tk) + kt[s], 0)),
                  pl.BlockSpec((MLA_KV_LORA, tk), lambda b, s, qt, kt: (0, b * (S // tk) + kt[s])),
                  pl.BlockSpec(wuvT.shape, lambda b, s, qt, kt: (0, 0, 0))],
        out_specs=pl.BlockSpec((tq, H * MLA_V), lambda b, s, qt, kt: (b * nq + qt[s], 0)),
        scratch_shapes=[pltpu.VMEM((1, H * tq), F32), pltpu.VMEM((1, H * tq), F32),
                        pltpu.VMEM((MLA_KV_LORA, H * tq), F32)],
    )
    return pl.pallas_call(
        functools.partial(_mla_prompt_kernel, tq=tq, tk=tk),
        out_shape=jax.ShapeDtypeStruct((B * S, H * MLA_V), BF16),
        grid_spec=grid_spec,
        compiler_params=_cp("parallel", "arbitrary"),
        name="mla_prompt",
    )(qi_tab, ki_tab, qT, kp, ckvT, wuvT)


def _sort_key(score):
    bits = lax.bitcast_convert_type(score, I32)
    key = jnp.where(bits < 0, bits ^ 0x7FFFFFFF, bits)
    return jnp.where(score == 0.0, 0, key)


def _radix_threshold(count_ge, shape, n_sel):
    def bit_body(i, w):
        candw = w | lax.shift_left(jnp.int32(1), 31 - i)
        return jnp.where(count_ge(candw ^ INT_MIN) >= n_sel, candw, w)
    return lax.fori_loop(0, 32, bit_body, jnp.zeros(shape, I32)) ^ INT_MIN


def _select_rows(keys_ref, am_ref, n_sel):
    rows, width = keys_ref.shape
    nch = width // LANES

    def count(pred):
        parts = [jnp.zeros((rows, LANES), F32) for _ in range(min(COUNT_CHAINS, nch))]
        for c in range(nch):
            hit = jnp.where(pred(keys_ref[:, c * LANES:(c + 1) * LANES]), 1.0, 0.0)
            parts[c % len(parts)] = parts[c % len(parts)] + hit
        cnt = parts[0]
        for part in parts[1:]:
            cnt = cnt + part
        return jnp.sum(cnt, axis=-1, keepdims=True)

    bc = lambda t: jnp.broadcast_to(t, (rows, LANES))
    thr = _radix_threshold(lambda t: count(lambda k: k >= bc(t)), (rows, 1), n_sel)
    thr_b = bc(thr)
    n_gt = count(lambda k: k > thr_b)
    n_ge = count(lambda k: k >= thr_b)
    tied = jnp.max(jnp.where((n_ge > n_sel) & (thr != KEY_NEG), 1.0, 0.0)) > 0.0

    @pl.when(jnp.logical_not(tied))
    def _():
        for c in range(nch):
            k = keys_ref[:, c * LANES:(c + 1) * LANES]
            am_ref[:, c * LANES:(c + 1) * LANES] = jnp.where((k >= thr_b) & (k != KEY_NEG), 0.0, NEG_INF)

    @pl.when(tied)
    def _():
        need = bc(n_sel - n_gt)
        jj = lax.broadcasted_iota(I32, (LANES, LANES), 0)
        cc = lax.broadcasted_iota(I32, (LANES, LANES), 1)
        before = jnp.where(jj < cc, 1.0, 0.0).astype(BF16)
        ones = jnp.ones((LANES, LANES), BF16)

        def body(c, base):
            sl = pl.ds(pl.multiple_of(c * LANES, LANES), LANES)
            k = keys_ref[:, sl]
            eq = k == thr_b
            eqb = jnp.where(eq, 1.0, 0.0).astype(BF16)
            rank = base + _dot(eqb, before)
            sel = ((k > thr_b) | (eq & (rank < need))) & (k != KEY_NEG)
            am_ref[:, sl] = jnp.where(sel, 0.0, NEG_INF)
            return base + _dot(eqb, ones)
        lax.fori_loop(0, nch, body, jnp.zeros((rows, LANES), F32))


def _select_cols(keys_ref, am_ref, nblk, blk, n_sel):
    Q = keys_ref.shape[1]
    acc_rows = COUNT_CHAINS * SUBLANES
    fold = lambda x: jnp.sum(x.reshape(blk // acc_rows, acc_rows, Q), axis=0)

    def count(pred):
        def body(b, cnt):
            k = keys_ref[pl.ds(pl.multiple_of(b * blk, blk), blk), :]
            return cnt + fold(jnp.where(pred(k), 1.0, 0.0))
        cnt = lax.fori_loop(0, nblk, body, jnp.zeros((acc_rows, Q), F32))
        return jnp.sum(cnt, axis=0, keepdims=True)

    bc = lambda t: jnp.broadcast_to(t, (blk, Q))
    thr = _radix_threshold(lambda t: count(lambda k: k >= bc(t)), (1, Q), n_sel)
    thr_b = bc(thr)
    n_gt = count(lambda k: k > thr_b)
    n_ge = count(lambda k: k >= thr_b)
    tied = jnp.max(jnp.where((n_ge > n_sel) & (thr != KEY_NEG), 1.0, 0.0)) > 0.0

    @pl.when(jnp.logical_not(tied))
    def _():
        def body(b, carry):
            sl = pl.ds(pl.multiple_of(b * blk, blk), blk)
            k = keys_ref[sl, :]
            am_ref[sl, :] = jnp.where((k >= thr_b) & (k != KEY_NEG), 0.0, NEG_INF)
            return carry
        lax.fori_loop(0, nblk, body, 0)

    @pl.when(tied)
    def _():
        need = bc(n_sel - n_gt)
        rr = lax.broadcasted_iota(I32, (blk, blk), 0)
        jj = lax.broadcasted_iota(I32, (blk, blk), 1)
        before = jnp.where(jj < rr, 1.0, 0.0).astype(BF16)

        def body(b, base):
            sl = pl.ds(pl.multiple_of(b * blk, blk), blk)
            k = keys_ref[sl, :]
            eq = k == thr_b
            eqf = jnp.where(eq, 1.0, 0.0)
            rank = base + _dot(before, eqf.astype(BF16))
            sel = ((k > thr_b) | (eq & (rank < need))) & (k != KEY_NEG)
            am_ref[sl, :] = jnp.where(sel, 0.0, NEG_INF)
            return base + jnp.sum(eqf, axis=0, keepdims=True)
        lax.fori_loop(0, nblk, body, jnp.zeros((1, Q), F32))


def _dsa_prompt_kernel(qcT_ref, qiT_ref, wiT_ref, kc_ref, vT_ref, ki_ref, bias_ref, o_ref,
                       rq_ref, keys_ref, am_ref, m_ref, l_ref, acc_ref, *, tq, tk, n_sel):
    qidx = pl.program_id(1)
    nkb = (qidx * tq + tq - 1) // tk + 1
    H, G, Dh = DSA_HEADS, DSA_GROUP, DSA_HEAD_DIM

    zeros = jnp.zeros((Dh, tq), BF16)
    for h in range(H):
        piece = qcT_ref[h * Dh:(h + 1) * Dh, :]
        rq_ref[:, h * tq:(h + 1) * tq] = jnp.concatenate(
            [piece, zeros] if h // G == 0 else [zeros, piece], axis=0)
    ri = jnp.concatenate([jnp.concatenate([qiT_ref[hh * IDX_DIM:(hh + 1) * IDX_DIM, :], zeros], axis=0)
                          for hh in range(IDX_HEADS)], axis=1)
    wiT = wiT_ref[...]

    krow = lax.broadcasted_iota(I32, (tk, tq), 0)
    qpos = qidx * tq + lax.broadcasted_iota(I32, (tk, tq), 1)

    def p1(kb, carry):
        ks = pl.multiple_of(kb * tk, tk)
        d = jnp.maximum(_dot(ki_ref[pl.ds(ks, tk), :], ri), 0.0)
        sc = d[:, 0:tq] * wiT[0:1, :]
        for hh in range(1, IDX_HEADS):
            sc = sc + d[:, hh * tq:(hh + 1) * tq] * wiT[hh:hh + 1, :]
        keys_ref[pl.ds(ks, tk), :] = jnp.where(kb * tk + krow <= qpos, _sort_key(sc), KEY_NEG)
        return carry
    lax.fori_loop(0, nkb, p1, 0)

    _select_cols(keys_ref, am_ref, nkb, tk, n_sel)

    m_ref[...] = jnp.full(m_ref.shape, NEG_INF, F32)
    l_ref[...] = jnp.zeros(l_ref.shape, F32)
    acc_ref[...] = jnp.zeros(acc_ref.shape, F32)

    def p3(kb, carry):
        ks = pl.multiple_of(kb * tk, tk)
        ty = jnp.minimum((qidx * tq - kb * tk) // tq, bias_ref.shape[0] - 1)
        bias = jnp.concatenate([bias_ref[ty, h] for h in range(H)], axis=1)
        s = _dot(kc_ref[pl.ds(ks, tk), :], rq_ref[...]) + bias + _lane_tile(am_ref[pl.ds(ks, tk), :], H)
        m_old = m_ref[...]
        m_new = jnp.maximum(m_old, jnp.max(s, axis=0, keepdims=True))
        m_use = jnp.where(m_new == NEG_INF, 0.0, m_new)
        alpha = jnp.exp(m_old - m_use)
        p = jnp.exp(s - m_use)
        l_ref[...] = alpha * l_ref[...] + jnp.sum(p, axis=0, keepdims=True)
        acc_ref[...] = alpha * acc_ref[...] + _dot(vT_ref[:, pl.ds(ks, tk)], p.astype(BF16))
        m_ref[...] = m_new
        return carry
    lax.fori_loop(0, nkb, p3, 0)

    oT = acc_ref[...] / l_ref[...]
    heads = [oT[(h // G) * Dh:(h // G + 1) * Dh, h * tq:(h + 1) * tq] for h in range(H)]
    o_ref[...] = jnp.concatenate(heads, axis=0).T.astype(o_ref.dtype)


def _dsa_prompt(qcT, qiT, wiT, kcb, vT, kib, biasT, B, S):
    tq, tk = ATT_TQ, ATT_TK
    nq = S // tq
    n_sel = min(DSA_TOPK_MAX, S // 4)
    H = DSA_HEADS
    qcol = lambda r: pl.BlockSpec((r, tq), lambda b, i: (0, b * nq + i))
    seq = pl.BlockSpec((S, LANES), lambda b, i: (b, 0))
    return pl.pallas_call(
        functools.partial(_dsa_prompt_kernel, tq=tq, tk=tk, n_sel=n_sel),
        out_shape=jax.ShapeDtypeStruct((B * S, H * DSA_HEAD_DIM), BF16),
        grid=(B, nq),
        in_specs=[qcol(512), qcol(256), qcol(SUBLANES), seq,
                  pl.BlockSpec((LANES, S), lambda b, i: (0, b)), seq, _whole(biasT.shape)],
        out_specs=pl.BlockSpec((tq, H * DSA_HEAD_DIM), lambda b, i: (b * nq + i, 0)),
        scratch_shapes=[pltpu.VMEM((LANES, H * tq), BF16), pltpu.VMEM((S, tq), I32), pltpu.VMEM((S, tq), F32),
                        pltpu.VMEM((1, H * tq), F32), pltpu.VMEM((1, H * tq), F32),
                        pltpu.VMEM((LANES, H * tq), F32)],
        compiler_params=_cp("parallel", "arbitrary"),
        name="dsa_prompt",
    )(qcT, qiT, wiT, kcb, vT, kib, biasT)


def _lru_gates(xc, wa, ba, wg, bg, coef):
    xcb = xc.astype(BF16)
    r = jax.nn.sigmoid(_dot(xcb, wa) + ba)
    gi = jax.nn.sigmoid(_dot(xcb, wg) + bg)
    log_a = coef * r
    return jnp.exp(log_a), xc * gi, jnp.sqrt(1.0 - jnp.exp(2.0 * log_a))


def _lru_coef(lam):
    z = -lam
    return -LRU_C * (jnp.maximum(z, 0.0) + jnp.log1p(jnp.exp(-jnp.abs(z))))


def _rglru_prompt_kernel(x_ref, wc_ref, bc_ref, wa_ref, ba_ref, wg_ref, bg_ref, lam_ref, o_ref, hT_ref,
                         xpad_ref, a_ref, u_ref, *, chunk):
    S, W = x_ref.shape
    PAD = SUBLANES
    xpad_ref[0:PAD, :] = jnp.zeros((PAD, W), F32)
    coef = _lru_coef(lam_ref[...])
    wa = wa_ref[...]
    wg = wg_ref[...]
    for c in range(S // chunk):
        r0 = c * chunk
        xpad_ref[PAD + r0:PAD + r0 + chunk, :] = x_ref[r0:r0 + chunk, :]
        xc = bc_ref[...] + x_ref[r0:r0 + chunk, :] * wc_ref[CONV_W - 1:CONV_W, :]
        for k in range(CONV_W - 1):
            sh = CONV_W - 1 - k
            xc = xc + xpad_ref[PAD + r0 - sh:PAD + r0 - sh + chunk, :] * wc_ref[k:k + 1, :]
        a, xg, mult = _lru_gates(xc, wa, ba_ref[...], wg, bg_ref[...], coef)
        if c == 0:
            first = lax.broadcasted_iota(I32, (chunk, W), 0) == 0
            a = jnp.where(first, 0.0, a)
            mult = jnp.where(first, 1.0, mult)
        a_ref[r0:r0 + chunk, :] = a
        u_ref[r0:r0 + chunk, :] = xg * mult

    rowi = lax.broadcasted_iota(I32, (SUBLANES, W), 0)

    def scan8(r, h):
        a8 = a_ref[pl.ds(r, SUBLANES), :]
        u8 = u_ref[pl.ds(r, SUBLANES), :]
        for s in (1, 2, 4):
            keep = rowi >= s
            u8 = jnp.where(keep, a8 * pltpu.roll(u8, s, axis=0) + u8, u8)
            a8 = jnp.where(keep, a8 * pltpu.roll(a8, s, axis=0), a8)
        return u8 + a8 * h

    def body(i, h):
        r = pl.multiple_of(i * 16, 16)
        h0 = scan8(r, h)
        h1 = scan8(r + SUBLANES, h0[SUBLANES - 1:SUBLANES, :])
        o_ref[pl.ds(r, 16), :] = jnp.concatenate([h0, h1], axis=0).astype(o_ref.dtype)
        return h1[SUBLANES - 1:SUBLANES, :]

    hT_ref[...] = lax.fori_loop(0, S // 16, body, jnp.zeros((1, W), F32))


def _rglru_prompt(xr, lw, B, S):
    W = LRU_WIDTH
    vec = _whole((1, W))
    return pl.pallas_call(
        functools.partial(_rglru_prompt_kernel, chunk=256),
        out_shape=[jax.ShapeDtypeStruct((B * S, W), BF16), jax.ShapeDtypeStruct((B, 1, W), F32)],
        grid=(B,),
        in_specs=[pl.BlockSpec((S, W), lambda b: (b, 0)), _whole((CONV_W, W)), vec,
                  _whole((W, W)), vec, _whole((W, W)), vec, vec],
        out_specs=[pl.BlockSpec((S, W), lambda b: (b, 0)), pl.BlockSpec((None, 1, W), lambda b: (b, 0, 0))],
        scratch_shapes=[pltpu.VMEM((S + SUBLANES, W), F32), pltpu.VMEM((S, W), F32), pltpu.VMEM((S, W), F32)],
        compiler_params=_cp("parallel"),
        name="rglru_prompt",
    )(xr, lw["w_conv"], lw["b_conv"], lw["w_rg_a"], lw["b_rg_a"], lw["w_rg_i"], lw["b_rg_i"], lw["lru_lambda"])


def _rglru_sample_kernel(x_ref, cb_ref, h0_ref, wc_ref, bc_ref, wa_ref, ba_ref, wg_ref, bg_ref, lam_ref,
                         o_ref, hT_ref):
    T = x_ref.shape[0]
    xp = [cb_ref[k] for k in range(CONV_W - 1)] + [x_ref[t] for t in range(T)]
    coef = _lru_coef(lam_ref[...])
    h = h0_ref[...]
    for t in range(T):
        xc = bc_ref[...]
        for k in range(CONV_W):
            xc = xc + xp[t + k] * wc_ref[k:k + 1, :]
        a, xg, mult = _lru_gates(xc, wa_ref[...], ba_ref[...], wg_ref[...], bg_ref[...], coef)
        h = a * h + xg * mult
        o_ref[t] = h.astype(o_ref.dtype)
    hT_ref[...] = h


def _rglru_sample(xr_t, conv_t, h0, lw):
    T, Bd, W = xr_t.shape
    return pl.pallas_call(
        _rglru_sample_kernel,
        out_shape=[jax.ShapeDtypeStruct((T, Bd, W), BF16), jax.ShapeDtypeStruct((Bd, W), F32)],
        compiler_params=pltpu.CompilerParams(vmem_limit_bytes=VMEM_LIMIT),
        name="rglru_sample",
    )(xr_t, conv_t, h0, lw["w_conv"], lw["b_conv"], lw["w_rg_a"], lw["b_rg_a"], lw["w_rg_i"], lw["b_rg_i"],
      lw["lru_lambda"])


def _merge_kernel(x_ref, om_ref, ol_ref, od_ref, g_ref, gm_ref, wm_ref, wl_ref, wd_ref, wo_ref, o_ref):
    D = x_ref.shape[1]
    y = (g_ref[:, 0:D].astype(F32) * _dot(om_ref[...], wm_ref[...])
         + g_ref[:, D:2 * D].astype(F32) * _dot(ol_ref[...], wl_ref[...])
         + g_ref[:, 2 * D:3 * D].astype(F32) * _dot(od_ref[...], wd_ref[...]))
    o_ref[...] = x_ref[...] + gm_ref[...] * _dot(y.astype(BF16), wo_ref[...])


def _merge(x, o_mla, o_lru, o_dsa, gates, gate_mix, group_rows, lw):
    N, D = x.shape
    tm = ROW_TILE
    R = gate_mix.shape[1]
    tpg = group_rows // tm
    row = lambda w: pl.BlockSpec((tm, w), lambda i: (i, 0))
    return pl.pallas_call(
        _merge_kernel,
        out_shape=jax.ShapeDtypeStruct((N, D), F32),
        grid=(N // tm,),
        in_specs=[row(D), row(512), row(512), row(512), row(N_BRANCH * D),
                  pl.BlockSpec((None, R, D), lambda i: (i // tpg, 0, 0)),
                  _whole((512, D)), _whole((512, D)), _whole((512, D)), _whole((D, D))],
        out_specs=row(D),
        compiler_params=_cp("parallel"),
        name="merge",
    )(x, o_mla, o_lru, o_dsa, gates, gate_mix, lw["w_br_mla"], lw["w_br_lru"], lw["w_br_dsa"], lw["w_o"])


def _router_kernel(x_ref, gn_ref, sh_ref, sc_ref, wr_ref, br_ref, h_out, e_out, g_out):
    tm = x_ref.shape[0]
    h = _rms(x_ref[...], gn_ref[...]) * (1.0 + sc_ref[...]) + sh_ref[...]
    hb = h.astype(BF16)
    h_out[...] = hb
    logits = _dot(hb, wr_ref[...]) + br_ref[...]
    lane = lax.broadcasted_iota(I32, (tm, LANES), 1)
    lanef = lane.astype(F32)
    vals, idxs = [], []
    for _ in range(TOP_K):
        mx = jnp.max(logits, axis=-1, keepdims=True)
        ix = jnp.min(jnp.where(logits == mx, lanef, float(LANES)), axis=-1, keepdims=True)
        vals.append(mx)
        idxs.append(ix)
        logits = jnp.where(lanef == ix, NEG_INF, logits)
    ex = [jnp.exp(v - vals[0]) for v in vals]
    den = ex[0]
    for e in ex[1:]:
        den = den + e
    earr = jnp.zeros((tm, LANES), F32)
    garr = jnp.zeros((tm, LANES), F32)
    for k in range(TOP_K):
        earr = jnp.where(lane == k, idxs[k], earr)
        garr = jnp.where(lane == k, ex[k] / den, garr)
    e_out[...] = earr[:, 0:TOP_K].astype(I32)
    g_out[...] = garr[:, 0:TOP_K]


def _router(x, shift, scale, group_rows, lw):
    N, D = x.shape
    tm = ROW_TILE
    R = shift.shape[1]
    tpg = group_rows // tm
    row = lambda w: pl.BlockSpec((tm, w), lambda i: (i, 0))
    mod = pl.BlockSpec((None, R, D), lambda i: (i // tpg, 0, 0))
    return pl.pallas_call(
        _router_kernel,
        out_shape=[jax.ShapeDtypeStruct((N, D), BF16), jax.ShapeDtypeStruct((N, TOP_K), I32),
                   jax.ShapeDtypeStruct((N, TOP_K), F32)],
        grid=(N // tm,),
        in_specs=[row(D), _whole((1, D)), mod, mod, _whole((D, LANES)), _whole((1, LANES))],
        out_specs=[row(D), row(TOP_K), row(TOP_K)],
        compiler_params=_cp("parallel"),
        name="router",
    )(x, lw["g_norm_ffn"], shift, scale, lw["w_router"], lw["b_router"])


def _expert_kernel(be_ref, x_ref, wu_ref, bu_ref, wd_ref, bd_ref, o_ref):
    F = wd_ref.shape[0]
    z = _dot(x_ref[...], wu_ref[...].astype(BF16)) + bu_ref[...]
    glu = jnp.minimum(z[:, 0:F], SWIGLU_LIMIT)
    lin = jnp.clip(z[:, F:2 * F], -SWIGLU_LIMIT, SWIGLU_LIMIT)
    act = glu * jax.nn.sigmoid(SWIGLU_ALPHA * glu) * (lin + 1.0)
    o_ref[...] = (_dot(act.astype(BF16), wd_ref[...].astype(BF16)) + bd_ref[...]).astype(o_ref.dtype)


def _experts(xs, blk_e, w_up, b_up, w_down, b_down):
    Mp, D = xs.shape
    E, _, F2 = w_up.shape
    F = F2 // 2
    bm = MOE_BLOCK
    grid_spec = pltpu.PrefetchScalarGridSpec(
        num_scalar_prefetch=1,
        grid=(Mp // bm,),
        in_specs=[pl.BlockSpec((bm, D), lambda j, be: (j, 0)),
                  pl.BlockSpec((None, D, F2), lambda j, be: (be[j], 0, 0)),
                  pl.BlockSpec((None, 1, F2), lambda j, be: (be[j], 0, 0)),
                  pl.BlockSpec((None, F, D), lambda j, be: (be[j], 0, 0)),
                  pl.BlockSpec((None, 1, D), lambda j, be: (be[j], 0, 0))],
        out_specs=pl.BlockSpec((bm, D), lambda j, be: (j, 0)),
    )
    return pl.pallas_call(
        _expert_kernel,
        out_shape=jax.ShapeDtypeStruct((Mp, D), BF16),
        grid_spec=grid_spec,
        compiler_params=_cp("arbitrary"),
        name="experts",
    )(blk_e, xs, w_up, b_up.reshape(E, 1, F2), w_down, b_down.reshape(E, 1, D))


def _combine_kernel(x_ref, y_ref, g_ref, gf_ref, gfin_ref, o_ref, *, final):
    g = g_ref[...]
    y = g[:, 0:1] * y_ref[0].astype(F32)
    for k in range(1, TOP_K):
        y = y + g[:, k:k + 1] * y_ref[k].astype(F32)
    out = x_ref[...] + gf_ref[...] * y
    if final:
        out = _rms(out, gfin_ref[...])
    o_ref[...] = out


def _combine(x, y4, gate, row0, gate_ffn, group_rows, g_final, final):
    N, D = x.shape
    tm = ROW_TILE
    R = gate_ffn.shape[1]
    tpg = group_rows // tm
    t0 = row0 // tm
    row = lambda w: pl.BlockSpec((tm, w), lambda i: (i, 0))
    return pl.pallas_call(
        functools.partial(_combine_kernel, final=final),
        out_shape=jax.ShapeDtypeStruct((N, D), F32),
        grid=(N // tm,),
        in_specs=[row(D), pl.BlockSpec((TOP_K, tm, D), lambda i: (0, i + t0, 0)),
                  pl.BlockSpec((tm, TOP_K), lambda i: (i + t0, 0)),
                  pl.BlockSpec((None, R, D), lambda i: (i // tpg, 0, 0)), _whole((1, D))],
        out_specs=row(D),
        compiler_params=_cp("parallel"),
        name="combine",
    )(x, y4, gate, gate_ffn, g_final)


def _page_specs(layer, width, npg, nkt):
    def spec(i):
        return pl.BlockSpec((None, None, LANES, width),
                            lambda b, j, pt: (layer, pt[b, jnp.minimum(j, nkt - 1) * npg + i], 0, 0))
    return [spec(i) for i in range(npg)]


def _online_update(s, pv, m_ref, l_ref, acc_ref):
    m_old = m_ref[...]
    m_new = jnp.maximum(m_old, jnp.max(s, axis=-1, keepdims=True))
    m_use = jnp.where(m_new == NEG_INF, 0.0, m_new)
    alpha = jnp.exp(m_old - m_use)
    p = jnp.exp(s - m_use)
    l_ref[...] = alpha * l_ref[...] + jnp.sum(p, axis=-1, keepdims=True)
    acc_ref[...] = alpha * acc_ref[...] + pv(p.astype(BF16))
    m_ref[...] = m_new


def _init_softmax(m_ref, l_ref, acc_ref):
    m_ref[...] = jnp.full(m_ref.shape, NEG_INF, F32)
    l_ref[...] = jnp.zeros(l_ref.shape, F32)
    acc_ref[...] = jnp.zeros(acc_ref.shape, F32)


def _paged_pv(p, pages):
    out = _dot(p[:, 0:LANES], pages[0])
    for i in range(1, len(pages)):
        out = out + _dot(p[:, i * LANES:(i + 1) * LANES], pages[i])
    return out


def _mla_sample_kernel(pt_ref, qa_ref, qp_ref, *rest, T, nkt, npg):
    lat_refs = rest[0:npg]
    kpe_refs = rest[npg:2 * npg]
    latn_ref, kpen_ref, o_ref, m_ref, l_ref, acc_ref = rest[2 * npg:]
    j = pl.program_id(1)

    @pl.when(j == 0)
    def _():
        _init_softmax(m_ref, l_ref, acc_ref)

    @pl.when(j < nkt)
    def _():
        lat = [r[...].astype(BF16) for r in lat_refs]
        s = jnp.concatenate([_dot_nt(qa_ref[...], lat[i]) + _dot_nt(qp_ref[...], kpe_refs[i][...].astype(BF16))
                             for i in range(npg)], axis=1)
        _online_update(s, lambda p: _paged_pv(p, lat), m_ref, l_ref, acc_ref)

    @pl.when(j == nkt)
    def _():
        R, C = qa_ref.shape[0], latn_ref.shape[0]
        latn = latn_ref[...].astype(BF16)
        s = _dot_nt(qa_ref[...], latn) + _dot_nt(qp_ref[...], kpen_ref[...].astype(BF16))
        t = lax.broadcasted_iota(I32, (R, C), 0) % T
        c = lax.broadcasted_iota(I32, (R, C), 1)
        _online_update(jnp.where(c <= t, s, NEG_INF), lambda p: _dot(p, latn), m_ref, l_ref, acc_ref)
        o_ref[...] = acc_ref[...] / l_ref[...]


def _mla_sample(page_table, qa, qp, cache_lat, cache_kpe, layer, latn, kpen, T):
    Bd, R, C = qa.shape
    n_pages = page_table.shape[1]
    npg = min(PAGES_PER_STEP, n_pages)
    nkt = n_pages // npg
    NP = latn.shape[1]
    per_b = lambda r, w: pl.BlockSpec((None, r, w), lambda b, j, pt: (b, 0, 0))
    grid_spec = pltpu.PrefetchScalarGridSpec(
        num_scalar_prefetch=1,
        grid=(Bd, nkt + 1),
        in_specs=[per_b(R, C), per_b(R, MLA_ROPE)] + _page_specs(layer, C, npg, nkt)
        + _page_specs(layer, MLA_ROPE, npg, nkt) + [per_b(NP, C), per_b(NP, MLA_ROPE)],
        out_specs=per_b(R, C),
        scratch_shapes=[pltpu.VMEM((R, 1), F32), pltpu.VMEM((R, 1), F32), pltpu.VMEM((R, C), F32)],
    )
    return pl.pallas_call(
        functools.partial(_mla_sample_kernel, T=T, nkt=nkt, npg=npg),
        out_shape=jax.ShapeDtypeStruct((Bd, R, C), F32),
        grid_spec=grid_spec,
        compiler_params=_cp("parallel", "arbitrary"),
        name="mla_sample",
    )(page_table, qa, qp, *([cache_lat] * npg), *([cache_kpe] * npg), latn, kpen)


def _idx_sample_kernel(pt_ref, qi_ref, wi_ref, *rest, T, TP, nkt, npg, n_sel):
    ki_refs = rest[0:npg]
    kin_ref, am_ref, keys_ref = rest[npg:]
    j = pl.program_id(1)
    wi = wi_ref[...]

    def keys_of(ki, valid):
        d = jnp.maximum(_dot_nt(qi_ref[...], ki), 0.0)
        sc = d[0:TP] * wi[:, 0:1]
        for hh in range(1, IDX_HEADS):
            sc = sc + d[hh * TP:(hh + 1) * TP] * wi[:, hh:hh + 1]
        return jnp.where(valid, _sort_key(sc), KEY_NEG)

    t = lax.broadcasted_iota(I32, (TP, LANES), 0)
    c = lax.broadcasted_iota(I32, (TP, LANES), 1)

    @pl.when(j < nkt)
    def _():
        key = jnp.concatenate([keys_of(r[...].astype(BF16), t < T) for r in ki_refs], axis=1)
        w = npg * LANES
        keys_ref[:, pl.ds(pl.multiple_of(j * w, w), w)] = key

    @pl.when(j == nkt)
    def _():
        keys_ref[:, nkt * npg * LANES:] = keys_of(kin_ref[...].astype(BF16), (t < T) & (c <= t))
        _select_rows(keys_ref, am_ref, n_sel)


def _idx_sample(page_table, qi, wi, cache_ki, layer, kin, T, n_sel):
    Bd, R, _ = qi.shape
    TP = wi.shape[1]
    n_pages = page_table.shape[1]
    npg = min(PAGES_PER_STEP, n_pages)
    nkt = n_pages // npg
    NP = kin.shape[1]
    assert NP == LANES
    width = n_pages * LANES + NP
    per_b = lambda r, w: pl.BlockSpec((None, r, w), lambda b, j, pt: (b, 0, 0))
    grid_spec = pltpu.PrefetchScalarGridSpec(
        num_scalar_prefetch=1,
        grid=(Bd, nkt + 1),
        in_specs=[per_b(R, IDX_DIM), per_b(TP, LANES)] + _page_specs(layer, IDX_DIM, npg, nkt)
        + [per_b(NP, IDX_DIM)],
        out_specs=per_b(TP, width),
        scratch_shapes=[pltpu.VMEM((TP, width), I32)],
    )
    return pl.pallas_call(
        functools.partial(_idx_sample_kernel, T=T, TP=TP, nkt=nkt, npg=npg, n_sel=n_sel),
        out_shape=jax.ShapeDtypeStruct((Bd, TP, width), F32),
        grid_spec=grid_spec,
        compiler_params=_cp("parallel", "arbitrary"),
        name="idx_sample",
    )(page_table, qi, wi, *([cache_ki] * npg), kin)


def _dsa_sample_kernel(pt_ref, q_ref, *rest, T, nkt, npg):
    k_refs = rest[0:npg]
    v_refs = rest[npg:2 * npg]
    kn_ref, vn_ref, am_ref, amn_ref, b_ref, bn_ref, o_ref, m_ref, l_ref, acc_ref = rest[2 * npg:]
    j = pl.program_id(1)
    H = DSA_HEADS

    @pl.when(j == 0)
    def _():
        _init_softmax(m_ref, l_ref, acc_ref)

    def rows_of(am):
        w = am.shape[1]
        return jnp.concatenate([jnp.broadcast_to(am[t:t + 1, :], (H, w)) for t in range(T)], axis=0)

    @pl.when(j < nkt)
    def _():
        s = jnp.concatenate([_dot_nt(q_ref[...], r[...].astype(BF16)) for r in k_refs], axis=1)
        s = s + b_ref[...] + rows_of(am_ref[...])
        _online_update(s, lambda p: _paged_pv(p, [r[...].astype(BF16) for r in v_refs]), m_ref, l_ref, acc_ref)

    @pl.when(j == nkt)
    def _():
        s = _dot_nt(q_ref[...], kn_ref[...].astype(BF16)) + bn_ref[...] + rows_of(amn_ref[...])
        _online_update(s, lambda p: _dot(p, vn_ref[...].astype(BF16)), m_ref, l_ref, acc_ref)
        o_ref[...] = acc_ref[...] / l_ref[...]


def _dsa_sample(page_table, q, cache_k, cache_v, layer, kn, vn, am, bias, T):
    Bd, R, C = q.shape
    n_pages = page_table.shape[1]
    npg = min(PAGES_PER_STEP, n_pages)
    nkt = n_pages // npg
    NP = kn.shape[1]
    TP = am.shape[1]
    tk = npg * LANES
    last = n_pages * LANES // NP
    per_b = lambda r, w: pl.BlockSpec((None, r, w), lambda b, j, pt: (b, 0, 0))
    grid_spec = pltpu.PrefetchScalarGridSpec(
        num_scalar_prefetch=1,
        grid=(Bd, nkt + 1),
        in_specs=[per_b(R, C)] + _page_specs(layer, C, npg, nkt) + _page_specs(layer, C, npg, nkt)
        + [per_b(NP, C), per_b(NP, C),
           pl.BlockSpec((None, TP, tk), lambda b, j, pt: (b, 0, jnp.minimum(j, nkt - 1))),
           pl.BlockSpec((None, TP, NP), lambda b, j, pt: (b, 0, last)),
           pl.BlockSpec((R, tk), lambda b, j, pt: (0, jnp.minimum(j, nkt - 1))),
           pl.BlockSpec((R, NP), lambda b, j, pt: (0, last))],
        out_specs=per_b(R, C),
        scratch_shapes=[pltpu.VMEM((R, 1), F32), pltpu.VMEM((R, 1), F32), pltpu.VMEM((R, C), F32)],
    )
    return pl.pallas_call(
        functools.partial(_dsa_sample_kernel, T=T, nkt=nkt, npg=npg),
        out_shape=jax.ShapeDtypeStruct((Bd, R, C), F32),
        grid_spec=grid_spec,
        compiler_params=_cp("parallel", "arbitrary"),
        name="dsa_sample",
    )(page_table, q, *([cache_k] * npg), *([cache_v] * npg), kn, vn, am, am, bias, bias)


def _rope_tables(pos):
    half = MLA_ROPE // 2
    freq = ROPE_THETA ** (-np.arange(half, dtype=np.float64) / half)
    ang = np.asarray(pos, np.float64)[:, None] * freq[None, :]
    cos = np.concatenate([np.cos(ang), np.cos(ang)], axis=1)
    sin = np.concatenate([-np.sin(ang), np.sin(ang)], axis=1)
    cos2 = np.tile(cos, (1, 2 * LANES // MLA_ROPE)).astype(np.float32)
    sin2 = np.tile(sin, (1, 2 * LANES // MLA_ROPE)).astype(np.float32)
    return (jnp.asarray(cos2[:, :LANES]), jnp.asarray(sin2[:, :LANES]),
            jnp.asarray(np.ascontiguousarray(cos2.T)), jnp.asarray(np.ascontiguousarray(sin2.T)))


def _block_diag(w):
    n, c, d = w.shape
    eye = jnp.eye(n, dtype=w.dtype)
    return (eye[:, None, :, None] * w[:, :, None, :]).reshape(n * c, n * d)


def _prep_layer(l, p):
    D = p["w_in"].shape[1]
    w = p["w_in"][l]
    sizes = (MLA_Q_LORA, MLA_KV_LORA, MLA_ROPE, LRU_WIDTH, DSA_HEADS * DSA_HEAD_DIM,
             DSA_KV_HEADS * DSA_HEAD_DIM, DSA_KV_HEADS * DSA_HEAD_DIM, IDX_HEADS * IDX_DIM, IDX_DIM,
             IDX_HEADS, N_BRANCH * D)
    offs = np.cumsum(sizes)[:-1].tolist()
    q_lat, kv_lat, kpe, xr, q_c, k_c, v_c, q_i, k_i, w_i, g = jnp.split(w, offs, axis=1)
    half = MLA_ROPE // 2
    kpe_sw = jnp.concatenate([kpe[:, half:], kpe[:, :half]], axis=1)
    padto = lambda a, n: jnp.pad(a, ((0, 0), (0, n - a.shape[1])))
    w_in = jnp.concatenate([kv_lat, jnp.tile(kpe, (1, ROPE_SLOT)), jnp.tile(kpe_sw, (1, ROPE_SLOT)),
                            xr, k_c, v_c, padto(k_i, LANES), g], axis=1).astype(BF16)
    w_inT = jnp.concatenate([q_lat, kv_lat, q_c, q_i, v_c, padto(w_i, SUBLANES)], axis=1).T.astype(BF16)
    assert w_in.shape[1] == _C_G + N_BRANCH * D and w_inT.shape[0] == _R_END
    w_uq = p["w_uq"][l]
    rope = w_uq[:, :, MLA_NOPE:]
    rope_sw = jnp.concatenate([rope[:, :, half:], rope[:, :, :half]], axis=2)
    row = lambda a: a.reshape(1, -1)
    col = lambda a: jnp.broadcast_to(a.reshape(-1, 1), (a.shape[0], LANES))
    E = p["w_router"].shape[2]
    return {
        "w_in": w_in, "w_inT": w_inT,
        "g_norm_mix": row(p["g_norm_mix"][l]), "g_norm_ffn": row(p["g_norm_ffn"][l]),
        "gqT": col(p["g_q_norm"][l]), "g_kv_norm": row(p["g_kv_norm"][l]), "gkvT": col(p["g_kv_norm"][l]),
        "wnT": jnp.transpose(w_uq[:, :, :MLA_NOPE], (1, 2, 0)).astype(BF16),
        "wrT": rope.reshape(MLA_Q_LORA, -1).T.astype(BF16),
        "wrsT": rope_sw.reshape(MLA_Q_LORA, -1).T.astype(BF16),
        "wukT": jnp.transpose(p["w_uk"][l], (1, 0, 2)).astype(BF16),
        "wuvT": jnp.transpose(p["w_uv"][l], (1, 2, 0)).astype(BF16),
        "bd_full": _block_diag(jnp.transpose(p["w_uv"][l], (1, 0, 2))).astype(BF16),
        "w_conv": p["w_conv"][l], "b_conv": row(p["b_conv"][l]),
        "w_rg_a": _block_diag(p["w_rg_a"][l]).astype(BF16), "b_rg_a": row(p["b_rg_a"][l]),
        "w_rg_i": _block_diag(p["w_rg_i"][l]).astype(BF16), "b_rg_i": row(p["b_rg_i"][l]),
        "lru_lambda": row(p["lru_lambda"][l]),
        "w_br_mla": p["w_br_mla"][l].astype(BF16), "w_br_lru": p["w_br_lru"][l].astype(BF16),
        "w_br_dsa": p["w_br_dsa"][l].astype(BF16), "w_o": p["w_o"][l].astype(BF16),
        "w_router": jnp.pad(p["w_router"][l], ((0, 0), (0, LANES - E))).astype(BF16),
        "b_router": jnp.pad(row(p["b_router"][l]), ((0, 0), (0, LANES - E)), constant_values=NEG_INF),
    }


def _moe(h_all, eidx, w_up, b_up, w_down, b_down):
    N, D = h_all.shape
    E = w_up.shape[0]
    bm = MOE_BLOCK
    M = N * TOP_K
    flat_e = eidx.reshape(M)
    order = jnp.argsort(flat_e)
    e_sorted = flat_e[order]
    counts = jnp.bincount(flat_e, length=E)
    padded = (counts + bm - 1) // bm * bm
    pad_end = jnp.cumsum(padded)
    pad_start = pad_end - padded
    start = jnp.cumsum(counts) - counts
    dest = pad_start[e_sorted] + jnp.arange(M) - start[e_sorted]
    n_blocks = M // bm + E
    blk_e = jnp.minimum(jnp.searchsorted(pad_end, jnp.arange(n_blocks) * bm, side="right"), E - 1).astype(I32)
    row_e = jnp.repeat(blk_e, bm)
    within = jnp.arange(n_blocks * bm) - pad_start[row_e]
    src = jnp.clip(start[row_e] + within, 0, M - 1)
    tok = jnp.where(within < counts[row_e], order[src] // TOP_K, 0).astype(I32)
    pos = dest[jnp.argsort(order)].astype(I32)
    xs = h_all[tok]
    out = _experts(xs, blk_e, w_up, b_up, w_down, b_down)
    return out[pos.reshape(N, TOP_K).T]


def kernel(x_prompt, x_sample, cache_mla_latent, cache_mla_kpe, cache_dsa_k, cache_dsa_v, cache_idx_k, state_lru_h, state_conv, page_table, c_prompt, c_sample, rel_bias, g_norm_mix, g_norm_ffn, w_ada, b_ada, w_in, g_q_norm, w_uq, g_kv_norm, w_uk, w_uv, w_conv, b_conv, w_rg_a, b_rg_a, w_rg_i, b_rg_i, lru_lambda, w_br_mla, w_br_lru, w_br_dsa, w_o, w_router, b_router, w_up, b_up, w_down, b_down, g_final):
    params = dict(g_norm_mix=g_norm_mix, g_norm_ffn=g_norm_ffn, w_in=w_in, g_q_norm=g_q_norm, w_uq=w_uq,
                  g_kv_norm=g_kv_norm, w_uk=w_uk, w_uv=w_uv, w_conv=w_conv, b_conv=b_conv, w_rg_a=w_rg_a,
                  b_rg_a=b_rg_a, w_rg_i=w_rg_i, b_rg_i=b_rg_i, lru_lambda=lru_lambda, w_br_mla=w_br_mla,
                  w_br_lru=w_br_lru, w_br_dsa=w_br_dsa, w_o=w_o, w_router=w_router, b_router=b_router)
    B, S, D = x_prompt.shape
    Bd, T, _ = x_sample.shape
    depth = w_in.shape[0]
    n_pages, page = page_table.shape[1], cache_mla_latent.shape[2]
    past = n_pages * page
    Np, Ns = B * S, Bd * T
    tm = ROW_TILE
    TP = SUBLANES
    NP = LANES
    assert page == LANES and S % max(tm, ATT_TK) == 0 and Ns % tm == 0 and T <= TP
    assert n_pages % min(PAGES_PER_STEP, n_pages) == 0 and tm % T == 0
    H = MLA_HEADS

    rope_p = _rope_tables(np.arange(S))
    rope_s = _rope_tables(np.tile(past + np.arange(T), tm // T))
    ntypes = -(-(_BUCKET_FAR + ATT_TK - 1) // ATT_TQ) + 1
    bias_p = _bias_tiles(rel_bias, ntypes, ATT_TK, ATT_TQ, 0, ATT_TQ, -1, 1, "bias_prompt")
    bias_s = _bias_tiles(rel_bias, T, SUBLANES, past + NP, past, 1, 0, -1, "bias_sample")
    bias_s = bias_s[:, :, 0, :].reshape(T * DSA_HEADS, past + NP)
    n_sel_s = min(DSA_TOPK_MAX, (past + T) // 4)
    cache_k2 = cache_dsa_k.reshape(cache_dsa_k.shape[:3] + (-1,))
    cache_v2 = cache_dsa_v.reshape(cache_dsa_v.shape[:3] + (-1,))

    xp = x_prompt.reshape(Np, D)
    xs = x_sample.reshape(Ns, D)
    c_all = jnp.concatenate([c_prompt, c_sample], axis=0)
    new_p = [[] for _ in range(7)]
    new_s = [[] for _ in range(7)]
    per_tok = lambda a: jnp.repeat(a, T, axis=0).reshape(Ns // tm, tm, D)
    for l in range(depth):
        lw = _prep_layer(l, params)
        ada = _matmul(c_all, w_ada[l], b_ada[l].reshape(1, -1), D, "ada")
        ada_p = [a.reshape(B, 1, D) for a in jnp.split(ada[:B], 6, axis=1)]
        ada_s = [per_tok(a) for a in jnp.split(ada[B:], 6, axis=1)]

        (qT, ckv, kp, ckvT, kpe, xr, qcT, kc, vc, kcb, vT, qiT, ki, kib, wiT, gates) = _in_proj(
            xp, ada_p[0], ada_p[1], S, rope_p, lw)
        o_mla = _mla_prompt(qT, kp, ckvT, lw["wuvT"], B, S)
        o_lru, hT = _rglru_prompt(xr, lw, B, S)
        o_dsa = _dsa_prompt(qcT, qiT, wiT, kcb, vT, kib, bias_p, B, S)
        xp = _merge(xp, o_mla, o_lru, o_dsa, gates, ada_p[2], S, lw)
        xr3 = xr.reshape(B, S, LRU_WIDTH)
        conv_p = jnp.concatenate([jnp.zeros((B, CONV_W - 1, LRU_WIDTH), F32), xr3], axis=1)[:, S:]
        for i, a in enumerate((ckv.reshape(B, S, -1), kpe[:, :MLA_ROPE].reshape(B, S, -1),
                               kc.reshape(B, S, DSA_KV_HEADS, DSA_HEAD_DIM),
                               vc.reshape(B, S, DSA_KV_HEADS, DSA_HEAD_DIM),
                               ki[:, :IDX_DIM].reshape(B, S, -1), hT.reshape(B, -1), conv_p)):
            new_p[i].append(a)

        (qT, ckv, kp, ckvT, kpe, xr, qcT, kc, vc, kcb, vT, qiT, ki, kib, wiT, gates) = _in_proj(
            xs, ada_s[0], ada_s[1], tm, rope_s, lw)
        q4 = jnp.transpose(qT.reshape(QK_W, Ns // LANES, H, LANES), (2, 1, 3, 0)).reshape(H, Bd, T, QK_W)
        q4 = jnp.transpose(q4, (1, 0, 2, 3))
        qa = q4[..., :MLA_KV_LORA].reshape(Bd, H * T, MLA_KV_LORA)
        qp = jnp.stack([q4[:, h, :, MLA_KV_LORA + (h % ROPE_SLOT) * MLA_ROPE:
                           MLA_KV_LORA + (h % ROPE_SLOT + 1) * MLA_ROPE] for h in range(H)], axis=1)
        qp = qp.reshape(Bd, H * T, MLA_ROPE)
        padk = lambda a: jnp.pad(a.reshape(Bd, T, -1), ((0, 0), (0, NP - T), (0, 0)))
        o_lat = _mla_sample(page_table, qa, qp, cache_mla_latent, cache_mla_kpe, l,
                            padk(ckv), padk(kpe[:, :MLA_ROPE]), T)
        o_lat = jnp.transpose(o_lat.reshape(Bd, H, T, MLA_KV_LORA), (0, 2, 1, 3)).reshape(Ns, H * MLA_KV_LORA)
        o_mla = _matmul(o_lat, lw["bd_full"], jnp.zeros((1, H * MLA_V), F32), H * MLA_V,
                        "mla_sample_out").astype(BF16)

        xr_t = jnp.transpose(xr.reshape(Bd, T, LRU_WIDTH), (1, 0, 2))
        o_lru_t, hT = _rglru_sample(xr_t, jnp.transpose(state_conv[l], (1, 0, 2)), state_lru_h[l], lw)
        o_lru = jnp.transpose(o_lru_t, (1, 0, 2)).reshape(Ns, LRU_WIDTH)
        conv_s = jnp.concatenate([state_conv[l], xr.reshape(Bd, T, LRU_WIDTH)], axis=1)[:, T:]

        qi_s = jnp.transpose(qiT.reshape(IDX_HEADS, IDX_DIM, Bd, T), (2, 0, 3, 1))
        qi_s = jnp.pad(qi_s, ((0, 0), (0, 0), (0, TP - T), (0, 0))).reshape(Bd, IDX_HEADS * TP, IDX_DIM)
        wi_s = jnp.transpose(wiT.reshape(SUBLANES, Bd, T), (1, 2, 0))
        wi_s = jnp.pad(wi_s, ((0, 0), (0, TP - T), (0, LANES - SUBLANES)))
        am = _idx_sample(page_table, qi_s, wi_s, cache_idx_k, l, padk(ki[:, :IDX_DIM]), T, n_sel_s)
        qc4 = jnp.transpose(qcT.reshape(DSA_KV_HEADS, DSA_GROUP, DSA_HEAD_DIM, Bd, T), (3, 4, 0, 1, 2))
        zeros = jnp.zeros_like(qc4)
        q_s = jnp.stack([jnp.concatenate([qc4[:, :, 0], zeros[:, :, 0]], axis=-1),
                         jnp.concatenate([zeros[:, :, 1], qc4[:, :, 1]], axis=-1)], axis=2)
        q_s = q_s.reshape(Bd, T * DSA_HEADS, DSA_KV_HEADS * DSA_HEAD_DIM)
        o_s = _dsa_sample(page_table, q_s, cache_k2, cache_v2, l, padk(kc), padk(vc), am, bias_s, T)
        o_s = o_s.reshape(Bd, T, DSA_KV_HEADS, DSA_GROUP, DSA_KV_HEADS, DSA_HEAD_DIM)
        o_dsa = jnp.stack([o_s[:, :, g, :, g] for g in range(DSA_KV_HEADS)], axis=2)
        o_dsa = o_dsa.reshape(Ns, DSA_HEADS * DSA_HEAD_DIM).astype(BF16)
        xs = _merge(xs, o_mla, o_lru, o_dsa, gates, ada_s[2], tm, lw)
        for i, a in enumerate((ckv.reshape(Bd, T, -1), kpe[:, :MLA_ROPE].reshape(Bd, T, -1),
                               kc.reshape(Bd, T, DSA_KV_HEADS, DSA_HEAD_DIM),
                               vc.reshape(Bd, T, DSA_KV_HEADS, DSA_HEAD_DIM),
                               ki[:, :IDX_DIM].reshape(Bd, T, -1), hT, conv_s)):
            new_s[i].append(a)

        h_p, e_p, g_p = _router(xp, ada_p[3], ada_p[4], S, lw)
        h_s, e_s, g_s = _router(xs, ada_s[3], ada_s[4], tm, lw)
        gate = jnp.concatenate([g_p, g_s], axis=0)
        y4 = _moe(jnp.concatenate([h_p, h_s], axis=0), jnp.concatenate([e_p, e_s], axis=0),
                  w_up[l], b_up[l], w_down[l], b_down[l])
        final = l == depth - 1
        gfin = g_final.reshape(1, D)
        xp = _combine(xp, y4, gate, 0, ada_p[5], S, gfin, final)
        xs = _combine(xs, y4, gate, Np, ada_s[5], tm, gfin, final)

    outs = [xp.reshape(B, S, D), xs.reshape(Bd, T, D)]
    outs += [jnp.stack(a) for a in new_p] + [jnp.stack(a) for a in new_s]
    return tuple(outs)
```

```python
import functools
import math

import numpy as np
import jax
import jax.numpy as jnp
from jax import lax
from jax.experimental import pallas as pl
from jax.experimental.pallas import tpu as pltpu

F32 = jnp.float32
BF16 = jnp.bfloat16
I32 = jnp.int32

EPS = 1e-6
MLA_HEADS = 8
MLA_NOPE = 64
MLA_ROPE = 32
MLA_V = 64
MLA_Q_LORA = 384
MLA_KV_LORA = 256
MLA_SCALE = (MLA_NOPE + MLA_ROPE) ** -0.5
ROPE_THETA = 10000.0
LRU_WIDTH = 512
LRU_BLOCKS = 8
CONV_W = 4
LRU_C = 8.0
DSA_HEADS = 8
DSA_KV_HEADS = 2
DSA_GROUP = DSA_HEADS // DSA_KV_HEADS
DSA_HEAD_DIM = 64
DSA_SCALE = DSA_HEAD_DIM ** -0.5
IDX_HEADS = 4
IDX_DIM = 64
IDX_SCALE = IDX_DIM ** -0.5
DSA_TOPK_MAX = 256
N_BUCKETS = 32
MAX_DISTANCE = 128
N_EXPERTS = 32
TOP_K = 4
SWIGLU_ALPHA = 1.702
SWIGLU_LIMIT = 7.0
N_BRANCH = 3

LANES = 128
SUBLANES = 8
ROPE_SLOT = LANES // MLA_ROPE
QK_W = MLA_KV_LORA + LANES
VMEM_LIMIT = 56 * 1024 * 1024
INT_MIN = -2 ** 31
KEY_NEG = int(np.array(-np.inf, np.float32).view(np.int32)) ^ 0x7FFFFFFF
NEG_INF = float("-inf")

ROW_TILE = 256
ATT_TQ = LANES
ATT_TK = 256
MOE_BLOCK = 256
PAGES_PER_STEP = 32
COUNT_CHAINS = 4


def _cp(*sem):
    return pltpu.CompilerParams(dimension_semantics=sem, vmem_limit_bytes=VMEM_LIMIT)


def _whole(shape):
    nd = len(shape)
    return pl.BlockSpec(shape, lambda *a: (0,) * nd)


def _dot(a, b):
    return jnp.dot(a, b, preferred_element_type=F32)


def _dot_nt(a, b):
    return lax.dot_general(a, b, (((1,), (1,)), ((), ())), preferred_element_type=F32)


def _rms(x, g):
    return x * lax.rsqrt(jnp.mean(x * x, axis=-1, keepdims=True) + EPS) * g


def _rms_t(xT, gT):
    return xT * lax.rsqrt(jnp.mean(xT * xT, axis=0, keepdims=True) + EPS) * gT


def _lane_tile(x, n):
    return x if n == 1 else jnp.concatenate([x] * n, axis=1)


def _mm_kernel(x_ref, w_ref, b_ref, o_ref):
    o_ref[...] = _dot(x_ref[...].astype(BF16), w_ref[...].astype(BF16)) + b_ref[...]


def _matmul(x, w, b, tn, name):
    M, K = x.shape
    N = w.shape[1]
    return pl.pallas_call(
        _mm_kernel,
        out_shape=jax.ShapeDtypeStruct((M, N), F32),
        grid=(N // tn,),
        in_specs=[pl.BlockSpec((M, K), lambda j: (0, 0)),
                  pl.BlockSpec((K, tn), lambda j: (0, j)),
                  pl.BlockSpec((1, tn), lambda j: (0, j))],
        out_specs=pl.BlockSpec((M, tn), lambda j: (0, j)),
        compiler_params=_cp("parallel"),
        name=name,
    )(x, w, b)


def _bucket_starts():
    n = np.arange(0, 8 * MAX_DISTANCE)
    max_exact = N_BUCKETS // 2

    def buckets(dt):
        nf = np.maximum(n, 1).astype(dt)
        large = max_exact + (np.log(nf / dt(max_exact)) / dt(math.log(MAX_DISTANCE / max_exact))
                             * dt(N_BUCKETS - max_exact)).astype(np.int32)
        return np.where(n < max_exact, n, np.minimum(large, N_BUCKETS - 1))

    b64 = buckets(np.float64)
    assert (buckets(np.float32) == b64).all() and (np.diff(b64) >= 0).all()
    starts = [int(np.argmax(b64 >= k)) if (b64 >= k).any() else int(n[-1]) + 1 for k in range(N_BUCKETS)]
    far = int(np.argmax(b64 == N_BUCKETS - 1))
    assert (b64[far:] == N_BUCKETS - 1).all()
    return starts, far


_BUCKET_STARTS, _BUCKET_FAR = _bucket_starts()


def _bias_kernel(tab_ref, o_ref, *, delta0, delta_step, row_mult, col_mult):
    i = pl.program_id(0)
    R, C = o_ref.shape[2:]
    r = lax.broadcasted_iota(I32, (R, C), 0)
    c = lax.broadcasted_iota(I32, (R, C), 1)
    dist = delta0 + delta_step * i + row_mult * r + col_mult * c
    for h in range(DSA_HEADS):
        val = jnp.full((R, C), tab_ref[0, h], F32)
        for k in range(1, N_BUCKETS):
            val = jnp.where(dist >= _BUCKET_STARTS[k], tab_ref[k, h], val)
        o_ref[0, h] = val


def _bias_tiles(rel_bias, n, R, C, delta0, delta_step, row_mult, col_mult, name):
    return pl.pallas_call(
        functools.partial(_bias_kernel, delta0=delta0, delta_step=delta_step, row_mult=row_mult,
                          col_mult=col_mult),
        out_shape=jax.ShapeDtypeStruct((n, DSA_HEADS, R, C), F32),
        grid=(n,),
        in_specs=[pl.BlockSpec(memory_space=pltpu.SMEM)],
        out_specs=pl.BlockSpec((1, DSA_HEADS, R, C), lambda i: (i, 0, 0, 0)),
        compiler_params=_cp("parallel"),
        name=name,
    )(rel_bias)


_C_KV = 0
_C_KPE = _C_KV + MLA_KV_LORA
_C_KPES = _C_KPE + LANES
_C_XR = _C_KPES + LANES
_C_KC = _C_XR + LRU_WIDTH
_C_VC = _C_KC + LANES
_C_KI = _C_VC + LANES
_C_G = _C_KI + LANES
_R_QLAT = 0
_R_KV = _R_QLAT + MLA_Q_LORA
_R_QC = _R_KV + MLA_KV_LORA
_R_QI = _R_QC + DSA_HEADS * DSA_HEAD_DIM
_R_VC = _R_QI + IDX_HEADS * IDX_DIM
_R_WI = _R_VC + LANES
_R_END = _R_WI + SUBLANES


def _inproj_kernel(x_ref, gn_ref, sh_ref, sc_ref, cos_ref, sin_ref, cosT_ref, sinT_ref, w_ref, wT_ref,
                   gqT_ref, wnT_ref, wukT_ref, wrT_ref, wrsT_ref, gkv_ref, gkvT_ref,
                   qT_out, ckv_out, kp_out, ckvT_out, kpe_out, xr_out, qcT_out, kc_out, vc_out, kcb_out,
                   vT_out, qiT_out, ki_out, kib_out, wiT_out, gates_out):
    tm, D = x_ref.shape
    nl = tm // LANES
    hb = (_rms(x_ref[...], gn_ref[...]) * (1.0 + sc_ref[...]) + sh_ref[...]).astype(BF16)

    def seg(lo, width):
        return _dot(hb, w_ref[:, lo:lo + width])

    def seg_t(lo, rows):
        return _dot_nt(wT_ref[lo:lo + rows, :], hb)

    cqT = _rms_t(seg_t(_R_QLAT, MLA_Q_LORA), _lane_tile(gqT_ref[...], nl)).astype(BF16)
    qpeT = (_dot(wrT_ref[...], cqT) * cosT_ref[...] + _dot(wrsT_ref[...], cqT) * sinT_ref[...]) * MLA_SCALE
    sub = lax.broadcasted_iota(I32, (LANES, tm), 0)
    for h in range(MLA_HEADS):
        qnT = _dot(wnT_ref[h], cqT).astype(BF16)
        qaT = (_dot(wukT_ref[h], qnT) * MLA_SCALE).astype(BF16)
        blk = qpeT[(h // ROPE_SLOT) * LANES:(h // ROPE_SLOT + 1) * LANES, :]
        slot = h % ROPE_SLOT
        keep = (sub >= slot * MLA_ROPE) & (sub < (slot + 1) * MLA_ROPE)
        peT = jnp.where(keep, blk, 0.0).astype(BF16)
        for t in range(nl):
            c0 = (t * MLA_HEADS + h) * LANES
            qT_out[0:MLA_KV_LORA, c0:c0 + LANES] = qaT[:, t * LANES:(t + 1) * LANES]
            qT_out[MLA_KV_LORA:QK_W, c0:c0 + LANES] = peT[:, t * LANES:(t + 1) * LANES]

    ckv = _rms(seg(_C_KV, MLA_KV_LORA), gkv_ref[...])
    ckv_out[...] = ckv
    cos = cos_ref[...]
    sin = sin_ref[...]
    kpe = seg(_C_KPE, LANES) * cos + seg(_C_KPES, LANES) * sin
    kpe_out[...] = kpe
    kp_out[:, 0:MLA_KV_LORA] = ckv.astype(BF16)
    kp_out[:, MLA_KV_LORA:QK_W] = kpe.astype(BF16)
    ckvT_out[...] = _rms_t(seg_t(_R_KV, MLA_KV_LORA), _lane_tile(gkvT_ref[...], nl)).astype(BF16)

    xr_out[...] = seg(_C_XR, LRU_WIDTH)

    qcT_out[...] = (seg_t(_R_QC, DSA_HEADS * DSA_HEAD_DIM) * DSA_SCALE).astype(BF16)
    kc = seg(_C_KC, LANES)
    kc_out[...] = kc
    kcb_out[...] = kc.astype(BF16)
    vc_out[...] = seg(_C_VC, LANES)
    vT_out[...] = seg_t(_R_VC, LANES).astype(BF16)
    qiT_out[...] = (seg_t(_R_QI, IDX_HEADS * IDX_DIM) * IDX_SCALE).astype(BF16)
    ki = seg(_C_KI, LANES)
    ki_out[...] = ki
    kib_out[...] = ki.astype(BF16)
    wiT_out[...] = seg_t(_R_WI, SUBLANES) * (IDX_HEADS ** -0.5)

    for j in range(N_BRANCH):
        gates_out[:, j * D:(j + 1) * D] = jax.nn.sigmoid(seg(_C_G + j * D, D)).astype(BF16)


def _in_proj(x, shift, scale, group_rows, rope, lw):
    N, D = x.shape
    tm = ROW_TILE
    R = shift.shape[1]
    tiles_per_group = group_rows // tm
    cos, sin, cosT, sinT = rope
    ptiles = cos.shape[0] // tm
    row = lambda w: pl.BlockSpec((tm, w), lambda i: (i, 0))
    col = lambda r: pl.BlockSpec((r, tm), lambda i: (0, i))
    mod = pl.BlockSpec((None, R, D), lambda i: (i // tiles_per_group, 0, 0))
    outs = [
        (jax.ShapeDtypeStruct((QK_W, N * MLA_HEADS), BF16),
         pl.BlockSpec((QK_W, tm * MLA_HEADS), lambda i: (0, i))),
        (jax.ShapeDtypeStruct((N, MLA_KV_LORA), F32), row(MLA_KV_LORA)),
        (jax.ShapeDtypeStruct((N, QK_W), BF16), row(QK_W)),
        (jax.ShapeDtypeStruct((MLA_KV_LORA, N), BF16), col(MLA_KV_LORA)),
        (jax.ShapeDtypeStruct((N, LANES), F32), row(LANES)),
        (jax.ShapeDtypeStruct((N, LRU_WIDTH), F32), row(LRU_WIDTH)),
        (jax.ShapeDtypeStruct((512, N), BF16), col(512)),
        (jax.ShapeDtypeStruct((N, LANES), F32), row(LANES)),
        (jax.ShapeDtypeStruct((N, LANES), F32), row(LANES)),
        (jax.ShapeDtypeStruct((N, LANES), BF16), row(LANES)),
        (jax.ShapeDtypeStruct((LANES, N), BF16), col(LANES)),
        (jax.ShapeDtypeStruct((256, N), BF16), col(256)),
        (jax.ShapeDtypeStruct((N, LANES), F32), row(LANES)),
        (jax.ShapeDtypeStruct((N, LANES), BF16), row(LANES)),
        (jax.ShapeDtypeStruct((SUBLANES, N), F32), col(SUBLANES)),
        (jax.ShapeDtypeStruct((N, N_BRANCH * D), BF16), row(N_BRANCH * D)),
    ]
    return pl.pallas_call(
        _inproj_kernel,
        out_shape=[o[0] for o in outs],
        grid=(N // tm,),
        in_specs=[row(D), _whole((1, D)), mod, mod,
                  pl.BlockSpec((tm, LANES), lambda i: (i % ptiles, 0)),
                  pl.BlockSpec((tm, LANES), lambda i: (i % ptiles, 0)),
                  pl.BlockSpec((2 * LANES, tm), lambda i: (0, i % ptiles)),
                  pl.BlockSpec((2 * LANES, tm), lambda i: (0, i % ptiles)),
                  _whole(lw["w_in"].shape), _whole(lw["w_inT"].shape), _whole(lw["gqT"].shape),
                  _whole(lw["wnT"].shape), _whole(lw["wukT"].shape), _whole(lw["wrT"].shape),
                  _whole(lw["wrsT"].shape), _whole((1, MLA_KV_LORA)), _whole(lw["gkvT"].shape)],
        out_specs=[o[1] for o in outs],
        compiler_params=_cp("parallel"),
        name="in_proj",
    )(x, lw["g_norm_mix"], shift, scale, cos, sin, cosT, sinT, lw["w_in"], lw["w_inT"], lw["gqT"],
      lw["wnT"], lw["wukT"], lw["wrT"], lw["wrsT"], lw["g_kv_norm"], lw["gkvT"])


def _mla_prompt_kernel(qi_tab, ki_tab, q_ref, k_ref, vT_ref, wuvT_ref, o_ref, m_ref, l_ref, acc_ref,
                       *, tq, tk):
    step = pl.program_id(1)
    qi = qi_tab[step]
    ki = ki_tab[step]
    last = (qi * tq + tq - 1) // tk
    H = MLA_HEADS

    @pl.when(ki == 0)
    def _():
        m_ref[...] = jnp.full(m_ref.shape, NEG_INF, F32)
        l_ref[...] = jnp.zeros(l_ref.shape, F32)
        acc_ref[...] = jnp.zeros(acc_ref.shape, F32)

    def update(masked):
        s = _dot(k_ref[...], q_ref[...])
        if masked:
            kpos = ki * tk + lax.broadcasted_iota(I32, (tk, H * tq), 0)
            qpos = qi * tq + (lax.broadcasted_iota(I32, (tk, H * tq), 1) & (tq - 1))
            s = jnp.where(kpos <= qpos, s, NEG_INF)
        m_old = m_ref[...]
        m_new = jnp.maximum(m_old, jnp.max(s, axis=0, keepdims=True))
        alpha = jnp.exp(m_old - m_new)
        p = jnp.exp(s - m_new)
        l_ref[...] = alpha * l_ref[...] + jnp.sum(p, axis=0, keepdims=True)
        acc_ref[...] = alpha * acc_ref[...] + _dot(vT_ref[...], p.astype(BF16))
        m_ref[...] = m_new

    @pl.when(ki < last)
    def _():
        update(False)

    @pl.when(ki == last)
    def _():
        update(True)
        oT = (acc_ref[...] / l_ref[...]).astype(BF16)
        heads = [_dot(wuvT_ref[h], oT[:, h * tq:(h + 1) * tq]) for h in range(H)]
        o_ref[...] = jnp.concatenate(heads, axis=0).T.astype(o_ref.dtype)


def _mla_prompt(qT, kp, ckvT, wuvT, B, S):
    tq, tk = ATT_TQ, ATT_TK
    nq = S // tq
    pairs = [(qi, ki) for qi in range(nq) for ki in range((qi * tq + tq - 1) // tk + 1)]
    qi_tab = jnp.asarray(np.array([p[0] for p in pairs], np.int32))
    ki_tab = jnp.asarray(np.array([p[1] for p in pairs], np.int32))
    H = MLA_HEADS
    grid_spec = pltpu.PrefetchScalarGridSpec(
        num_scalar_prefetch=2,
        grid=(B, len(pairs)),
        in_specs=[pl.BlockSpec((QK_W, H * tq), lambda b, s, qt, kt: (0, b * nq + qt[s])),
                  pl.BlockSpec((tk, QK_W), lambda b, s, qt, kt: (b * (S // tk) + kt[s], 0)),
                  pl.BlockSpec((MLA_KV_LORA, tk), lambda b, s, qt, kt: (0, b * (S // tk) + kt[s])),
                  pl.BlockSpec(wuvT.shape, lambda b, s, qt, kt: (0, 0, 0))],
        out_specs=pl.BlockSpec((tq, H * MLA_V), lambda b, s, qt, kt: (b * nq + qt[s], 0)),
        scratch_shapes=[pltpu.VMEM((1, H * tq), F32), pltpu.VMEM((1, H * tq), F32),
                        pltpu.VMEM((MLA_KV_LORA, H * tq), F32)],
    )
    return pl.pallas_call(
        functools.partial(_mla_prompt_kernel, tq=tq, tk=tk),
        out_shape=jax.ShapeDtypeStruct((B * S, H * MLA_V), BF16),
        grid_spec=grid_spec,
        compiler_params=_cp("parallel", "arbitrary"),
        name="mla_prompt",
    )(qi_tab, ki_tab, qT, kp, ckvT, wuvT)


def _sort_key(score):
    bits = lax.bitcast_convert_type(score, I32)
    key = jnp.where(bits < 0, bits ^ 0x7FFFFFFF, bits)
    return jnp.where(score == 0.0, 0, key)


def _radix_threshold(count_ge, shape, n_sel):
    def bit_body(i, w):
        candw = w | lax.shift_left(jnp.int32(1), 31 - i)
        return jnp.where(count_ge(candw ^ INT_MIN) >= n_sel, candw, w)
    return lax.fori_loop(0, 32, bit_body, jnp.zeros(shape, I32)) ^ INT_MIN


def _select_rows(keys_ref, am_ref, n_sel):
    rows, width = keys_ref.shape
    nch = width // LANES

    def count(pred):
        parts = [jnp.zeros((rows, LANES), F32) for _ in range(min(COUNT_CHAINS, nch))]
        for c in range(nch):
            hit = jnp.where(pred(keys_ref[:, c * LANES:(c + 1) * LANES]), 1.0, 0.0)
            parts[c % len(parts)] = parts[c % len(parts)] + hit
        cnt = parts[0]
        for part in parts[1:]:
            cnt = cnt + part
        return jnp.sum(cnt, axis=-1, keepdims=True)

    bc = lambda t: jnp.broadcast_to(t, (rows, LANES))
    thr = _radix_threshold(lambda t: count(lambda k: k >= bc(t)), (rows, 1), n_sel)
    thr_b = bc(thr)
    n_gt = count(lambda k: k > thr_b)
    n_ge = count(lambda k: k >= thr_b)
    tied = jnp.max(jnp.where((n_ge > n_sel) & (thr != KEY_NEG), 1.0, 0.0)) > 0.0

    @pl.when(jnp.logical_not(tied))
    def _():
        for c in range(nch):
            k = keys_ref[:, c * LANES:(c + 1) * LANES]
            am_ref[:, c * LANES:(c + 1) * LANES] = jnp.where((k >= thr_b) & (k != KEY_NEG), 0.0, NEG_INF)

    @pl.when(tied)
    def _():
        need = bc(n_sel - n_gt)
        jj = lax.broadcasted_iota(I32, (LANES, LANES), 0)
        cc = lax.broadcasted_iota(I32, (LANES, LANES), 1)
        before = jnp.where(jj < cc, 1.0, 0.0).astype(BF16)
        ones = jnp.ones((LANES, LANES), BF16)

        def body(c, base):
            sl = pl.ds(pl.multiple_of(c * LANES, LANES), LANES)
            k = keys_ref[:, sl]
            eq = k == thr_b
            eqb = jnp.where(eq, 1.0, 0.0).astype(BF16)
            rank = base + _dot(eqb, before)
            sel = ((k > thr_b) | (eq & (rank < need))) & (k != KEY_NEG)
            am_ref[:, sl] = jnp.where(sel, 0.0, NEG_INF)
            return base + _dot(eqb, ones)
        lax.fori_loop(0, nch, body, jnp.zeros((rows, LANES), F32))


def _select_cols(keys_ref, am_ref, nblk, blk, n_sel):
    Q = keys_ref.shape[1]
    acc_rows = COUNT_CHAINS * SUBLANES
    fold = lambda x: jnp.sum(x.reshape(blk // acc_rows, acc_rows, Q), axis=0)

    def count(pred):
        def body(b, cnt):
            k = keys_ref[pl.ds(pl.multiple_of(b * blk, blk), blk), :]
            return cnt + fold(jnp.where(pred(k), 1.0, 0.0))
        cnt = lax.fori_loop(0, nblk, body, jnp.zeros((acc_rows, Q), F32))
        return jnp.sum(cnt, axis=0, keepdims=True)

    bc = lambda t: jnp.broadcast_to(t, (blk, Q))
    thr = _radix_threshold(lambda t: count(lambda k: k >= bc(t)), (1, Q), n_sel)
    thr_b = bc(thr)
    n_gt = count(lambda k: k > thr_b)
    n_ge = count(lambda k: k >= thr_b)
    tied = jnp.max(jnp.where((n_ge > n_sel) & (thr != KEY_NEG), 1.0, 0.0)) > 0.0

    @pl.when(jnp.logical_not(tied))
    def _():
        def body(b, carry):
            sl = pl.ds(pl.multiple_of(b * blk, blk), blk)
            k = keys_ref[sl, :]
            am_ref[sl, :] = jnp.where((k >= thr_b) & (k != KEY_NEG), 0.0, NEG_INF)
            return carry
        lax.fori_loop(0, nblk, body, 0)

    @pl.when(tied)
    def _():
        need = bc(n_sel - n_gt)
        rr = lax.broadcasted_iota(I32, (blk, blk), 0)
        jj = lax.broadcasted_iota(I32, (blk, blk), 1)
        before = jnp.where(jj < rr, 1.0, 0.0).astype(BF16)

        def body(b, base):
            sl = pl.ds(pl.multiple_of(b * blk, blk), blk)
            k = keys_ref[sl, :]
            eq = k == thr_b
            eqf = jnp.where(eq, 1.0, 0.0)
            rank = base + _dot(before, eqf.astype(BF16))
            sel = ((k > thr_b) | (eq & (rank < need))) & (k != KEY_NEG)
            am_ref[sl, :] = jnp.where(sel, 0.0, NEG_INF)
            return base + jnp.sum(eqf, axis=0, keepdims=True)
        lax.fori_loop(0, nblk, body, jnp.zeros((1, Q), F32))


def _dsa_prompt_kernel(qcT_ref, qiT_ref, wiT_ref, kc_ref, vT_ref, ki_ref, bias_ref, o_ref,
                       rq_ref, keys_ref, am_ref, m_ref, l_ref, acc_ref, *, tq, tk, n_sel):
    qidx = pl.program_id(1)
    nkb = (qidx * tq + tq - 1) // tk + 1
    H, G, Dh = DSA_HEADS, DSA_GROUP, DSA_HEAD_DIM

    zeros = jnp.zeros((Dh, tq), BF16)
    for h in range(H):
        piece = qcT_ref[h * Dh:(h + 1) * Dh, :]
        rq_ref[:, h * tq:(h + 1) * tq] = jnp.concatenate(
            [piece, zeros] if h // G == 0 else [zeros, piece], axis=0)
    ri = jnp.concatenate([jnp.concatenate([qiT_ref[hh * IDX_DIM:(hh + 1) * IDX_DIM, :], zeros], axis=0)
                          for hh in range(IDX_HEADS)], axis=1)
    wiT = wiT_ref[...]

    krow = lax.broadcasted_iota(I32, (tk, tq), 0)
    qpos = qidx * tq + lax.broadcasted_iota(I32, (tk, tq), 1)

    def p1(kb, carry):
        ks = pl.multiple_of(kb * tk, tk)
        d = jnp.maximum(_dot(ki_ref[pl.ds(ks, tk), :], ri), 0.0)
        sc = d[:, 0:tq] * wiT[0:1, :]
        for hh in range(1, IDX_HEADS):
            sc = sc + d[:, hh * tq:(hh + 1) * tq] * wiT[hh:hh + 1, :]
        keys_ref[pl.ds(ks, tk), :] = jnp.where(kb * tk + krow <= qpos, _sort_key(sc), KEY_NEG)
        return carry
    lax.fori_loop(0, nkb, p1, 0)

    _select_cols(keys_ref, am_ref, nkb, tk, n_sel)

    m_ref[...] = jnp.full(m_ref.shape, NEG_INF, F32)
    l_ref[...] = jnp.zeros(l_ref.shape, F32)
    acc_ref[...] = jnp.zeros(acc_ref.shape, F32)

    def p3(kb, carry):
        ks = pl.multiple_of(kb * tk, tk)
        ty = jnp.minimum((qidx * tq - kb * tk) // tq, bias_ref.shape[0] - 1)
        bias = jnp.concatenate([bias_ref[ty, h] for h in range(H)], axis=1)
        s = _dot(kc_ref[pl.ds(ks, tk), :], rq_ref[...]) + bias + _lane_tile(am_ref[pl.ds(ks, tk), :], H)
        m_old = m_ref[...]
        m_new = jnp.maximum(m_old, jnp.max(s, axis=0, keepdims=True))
        m_use = jnp.where(m_new == NEG_INF, 0.0, m_new)
        alpha = jnp.exp(m_old - m_use)
        p = jnp.exp(s - m_use)
        l_ref[...] = alpha * l_ref[...] + jnp.sum(p, axis=0, keepdims=True)
        acc_ref[...] = alpha * acc_ref[...] + _dot(vT_ref[:, pl.ds(ks, tk)], p.astype(BF16))
        m_ref[...] = m_new
        return carry
    lax.fori_loop(0, nkb, p3, 0)

    oT = acc_ref[...] / l_ref[...]
    heads = [oT[(h // G) * Dh:(h // G + 1) * Dh, h * tq:(h + 1) * tq] for h in range(H)]
    o_ref[...] = jnp.concatenate(heads, axis=0).T.astype(o_ref.dtype)


def _dsa_prompt(qcT, qiT, wiT, kcb, vT, kib, biasT, B, S):
    tq, tk = ATT_TQ, ATT_TK
    nq = S // tq
    n_sel = min(DSA_TOPK_MAX, S // 4)
    H = DSA_HEADS
    qcol = lambda r: pl.BlockSpec((r, tq), lambda b, i: (0, b * nq + i))
    seq = pl.BlockSpec((S, LANES), lambda b, i: (b, 0))
    return pl.pallas_call(
        functools.partial(_dsa_prompt_kernel, tq=tq, tk=tk, n_sel=n_sel),
        out_shape=jax.ShapeDtypeStruct((B * S, H * DSA_HEAD_DIM), BF16),
        grid=(B, nq),
        in_specs=[qcol(512), qcol(256), qcol(SUBLANES), seq,
                  pl.BlockSpec((LANES, S), lambda b, i: (0, b)), seq, _whole(biasT.shape)],
        out_specs=pl.BlockSpec((tq, H * DSA_HEAD_DIM), lambda b, i: (b * nq + i, 0)),
        scratch_shapes=[pltpu.VMEM((LANES, H * tq), BF16), pltpu.VMEM((S, tq), I32), pltpu.VMEM((S, tq), F32),
                        pltpu.VMEM((1, H * tq), F32), pltpu.VMEM((1, H * tq), F32),
                        pltpu.VMEM((LANES, H * tq), F32)],
        compiler_params=_cp("parallel", "arbitrary"),
        name="dsa_prompt",
    )(qcT, qiT, wiT, kcb, vT, kib, biasT)


def _lru_gates(xc, wa, ba, wg, bg, coef):
    xcb = xc.astype(BF16)
    r = jax.nn.sigmoid(_dot(xcb, wa) + ba)
    gi = jax.nn.sigmoid(_dot(xcb, wg) + bg)
    log_a = coef * r
    return jnp.exp(log_a), xc * gi, jnp.sqrt(1.0 - jnp.exp(2.0 * log_a))


def _lru_coef(lam):
    z = -lam
    return -LRU_C * (jnp.maximum(z, 0.0) + jnp.log1p(jnp.exp(-jnp.abs(z))))


def _rglru_prompt_kernel(x_ref, wc_ref, bc_ref, wa_ref, ba_ref, wg_ref, bg_ref, lam_ref, o_ref, hT_ref,
                         xpad_ref, a_ref, u_ref, *, chunk):
    S, W = x_ref.shape
    PAD = SUBLANES
    xpad_ref[0:PAD, :] = jnp.zeros((PAD, W), F32)
    coef = _lru_coef(lam_ref[...])
    wa = wa_ref[...]
    wg = wg_ref[...]
    for c in range(S // chunk):
        r0 = c * chunk
        xpad_ref[PAD + r0:PAD + r0 + chunk, :] = x_ref[r0:r0 + chunk, :]
        xc = bc_ref[...] + x_ref[r0:r0 + chunk, :] * wc_ref[CONV_W - 1:CONV_W, :]
        for k in range(CONV_W - 1):
            sh = CONV_W - 1 - k
            xc = xc + xpad_ref[PAD + r0 - sh:PAD + r0 - sh + chunk, :] * wc_ref[k:k + 1, :]
        a, xg, mult = _lru_gates(xc, wa, ba_ref[...], wg, bg_ref[...], coef)
        if c == 0:
            first = lax.broadcasted_iota(I32, (chunk, W), 0) == 0
            a = jnp.where(first, 0.0, a)
            mult = jnp.where(first, 1.0, mult)
        a_ref[r0:r0 + chunk, :] = a
        u_ref[r0:r0 + chunk, :] = xg * mult

    rowi = lax.broadcasted_iota(I32, (SUBLANES, W), 0)

    def scan8(r, h):
        a8 = a_ref[pl.ds(r, SUBLANES), :]
        u8 = u_ref[pl.ds(r, SUBLANES), :]
        for s in (1, 2, 4):
            keep = rowi >= s
            u8 = jnp.where(keep, a8 * pltpu.roll(u8, s, axis=0) + u8, u8)
            a8 = jnp.where(keep, a8 * pltpu.roll(a8, s, axis=0), a8)
        return u8 + a8 * h

    def body(i, h):
        r = pl.multiple_of(i * 16, 16)
        h0 = scan8(r, h)
        h1 = scan8(r + SUBLANES, h0[SUBLANES - 1:SUBLANES, :])
        o_ref[pl.ds(r, 16), :] = jnp.concatenate([h0, h1], axis=0).astype(o_ref.dtype)
        return h1[SUBLANES - 1:SUBLANES, :]

    hT_ref[...] = lax.fori_loop(0, S // 16, body, jnp.zeros((1, W), F32))


def _rglru_prompt(xr, lw, B, S):
    W = LRU_WIDTH
    vec = _whole((1, W))
    return pl.pallas_call(
        functools.partial(_rglru_prompt_kernel, chunk=256),
        out_shape=[jax.ShapeDtypeStruct((B * S, W), BF16), jax.ShapeDtypeStruct((B, 1, W), F32)],
        grid=(B,),
        in_specs=[pl.BlockSpec((S, W), lambda b: (b, 0)), _whole((CONV_W, W)), vec,
                  _whole((W, W)), vec, _whole((W, W)), vec, vec],
        out_specs=[pl.BlockSpec((S, W), lambda b: (b, 0)), pl.BlockSpec((None, 1, W), lambda b: (b, 0, 0))],
        scratch_shapes=[pltpu.VMEM((S + SUBLANES, W), F32), pltpu.VMEM((S, W), F32), pltpu.VMEM((S, W), F32)],
        compiler_params=_cp("parallel"),
        name="rglru_prompt",
    )(xr, lw["w_conv"], lw["b_conv"], lw["w_rg_a"], lw["b_rg_a"], lw["w_rg_i"], lw["b_rg_i"], lw["lru_lambda"])


def _rglru_sample_kernel(x_ref, cb_ref, h0_ref, wc_ref, bc_ref, wa_ref, ba_ref, wg_ref, bg_ref, lam_ref,
                         o_ref, hT_ref):
    T = x_ref.shape[0]
    xp = [cb_ref[k] for k in range(CONV_W - 1)] + [x_ref[t] for t in range(T)]
    coef = _lru_coef(lam_ref[...])
    h = h0_ref[...]
    for t in range(T):
        xc = bc_ref[...]
        for k in range(CONV_W):
            xc = xc + xp[t + k] * wc_ref[k:k + 1, :]
        a, xg, mult = _lru_gates(xc, wa_ref[...], ba_ref[...], wg_ref[...], bg_ref[...], coef)
        h = a * h + xg * mult
        o_ref[t] = h.astype(o_ref.dtype)
    hT_ref[...] = h


def _rglru_sample(xr_t, conv_t, h0, lw):
    T, Bd, W = xr_t.shape
    return pl.pallas_call(
        _rglru_sample_kernel,
        out_shape=[jax.ShapeDtypeStruct((T, Bd, W), BF16), jax.ShapeDtypeStruct((Bd, W), F32)],
        compiler_params=pltpu.CompilerParams(vmem_limit_bytes=VMEM_LIMIT),
        name="rglru_sample",
    )(xr_t, conv_t, h0, lw["w_conv"], lw["b_conv"], lw["w_rg_a"], lw["b_rg_a"], lw["w_rg_i"], lw["b_rg_i"],
      lw["lru_lambda"])


def _merge_kernel(x_ref, om_ref, ol_ref, od_ref, g_ref, gm_ref, wm_ref, wl_ref, wd_ref, wo_ref, o_ref):
    D = x_ref.shape[1]
    y = (g_ref[:, 0:D].astype(F32) * _dot(om_ref[...], wm_ref[...])
         + g_ref[:, D:2 * D].astype(F32) * _dot(ol_ref[...], wl_ref[...])
         + g_ref[:, 2 * D:3 * D].astype(F32) * _dot(od_ref[...], wd_ref[...]))
    o_ref[...] = x_ref[...] + gm_ref[...] * _dot(y.astype(BF16), wo_ref[...])


def _merge(x, o_mla, o_lru, o_dsa, gates, gate_mix, group_rows, lw):
    N, D = x.shape
    tm = ROW_TILE
    R = gate_mix.shape[1]
    tpg = group_rows // tm
    row = lambda w: pl.BlockSpec((tm, w), lambda i: (i, 0))
    return pl.pallas_call(
        _merge_kernel,
        out_shape=jax.ShapeDtypeStruct((N, D), F32),
        grid=(N // tm,),
        in_specs=[row(D), row(512), row(512), row(512), row(N_BRANCH * D),
                  pl.BlockSpec((None, R, D), lambda i: (i // tpg, 0, 0)),
                  _whole((512, D)), _whole((512, D)), _whole((512, D)), _whole((D, D))],
        out_specs=row(D),
        compiler_params=_cp("parallel"),
        name="merge",
    )(x, o_mla, o_lru, o_dsa, gates, gate_mix, lw["w_br_mla"], lw["w_br_lru"], lw["w_br_dsa"], lw["w_o"])


def _router_kernel(x_ref, gn_ref, sh_ref, sc_ref, wr_ref, br_ref, h_out, e_out, g_out):
    tm = x_ref.shape[0]
    h = _rms(x_ref[...], gn_ref[...]) * (1.0 + sc_ref[...]) + sh_ref[...]
    hb = h.astype(BF16)
    h_out[...] = hb
    logits = _dot(hb, wr_ref[...]) + br_ref[...]
    lane = lax.broadcasted_iota(I32, (tm, LANES), 1)
    lanef = lane.astype(F32)
    vals, idxs = [], []
    for _ in range(TOP_K):
        mx = jnp.max(logits, axis=-1, keepdims=True)
        ix = jnp.min(jnp.where(logits == mx, lanef, float(LANES)), axis=-1, keepdims=True)
        vals.append(mx)
        idxs.append(ix)
        logits = jnp.where(lanef == ix, NEG_INF, logits)
    ex = [jnp.exp(v - vals[0]) for v in vals]
    den = ex[0]
    for e in ex[1:]:
        den = den + e
    earr = jnp.zeros((tm, LANES), F32)
    garr = jnp.zeros((tm, LANES), F32)
    for k in range(TOP_K):
        earr = jnp.where(lane == k, idxs[k], earr)
        garr = jnp.where(lane == k, ex[k] / den, garr)
    e_out[...] = earr[:, 0:TOP_K].astype(I32)
    g_out[...] = garr[:, 0:TOP_K]


def _router(x, shift, scale, group_rows, lw):
    N, D = x.shape
    tm = ROW_TILE
    R = shift.shape[1]
    tpg = group_rows // tm
    row = lambda w: pl.BlockSpec((tm, w), lambda i: (i, 0))
    mod = pl.BlockSpec((None, R, D), lambda i: (i // tpg, 0, 0))
    return pl.pallas_call(
        _router_kernel,
        out_shape=[jax.ShapeDtypeStruct((N, D), BF16), jax.ShapeDtypeStruct((N, TOP_K), I32),
                   jax.ShapeDtypeStruct((N, TOP_K), F32)],
        grid=(N // tm,),
        in_specs=[row(D), _whole((1, D)), mod, mod, _whole((D, LANES)), _whole((1, LANES))],
        out_specs=[row(D), row(TOP_K), row(TOP_K)],
        compiler_params=_cp("parallel"),
        name="router",
    )(x, lw["g_norm_ffn"], shift, scale, lw["w_router"], lw["b_router"])


def _expert_kernel(be_ref, x_ref, wu_ref, bu_ref, wd_ref, bd_ref, o_ref):
    F = wd_ref.shape[0]
    z = _dot(x_ref[...], wu_ref[...].astype(BF16)) + bu_ref[...]
    glu = jnp.minimum(z[:, 0:F], SWIGLU_LIMIT)
    lin = jnp.clip(z[:, F:2 * F], -SWIGLU_LIMIT, SWIGLU_LIMIT)
    act = glu * jax.nn.sigmoid(SWIGLU_ALPHA * glu) * (lin + 1.0)
    o_ref[...] = (_dot(act.astype(BF16), wd_ref[...].astype(BF16)) + bd_ref[...]).astype(o_ref.dtype)


def _experts(xs, blk_e, w_up, b_up, w_down, b_down):
    Mp, D = xs.shape
    E, _, F2 = w_up.shape
    F = F2 // 2
    bm = MOE_BLOCK
    grid_spec = pltpu.PrefetchScalarGridSpec(
        num_scalar_prefetch=1,
        grid=(Mp // bm,),
        in_specs=[pl.BlockSpec((bm, D), lambda j, be: (j, 0)),
                  pl.BlockSpec((None, D, F2), lambda j, be: (be[j], 0, 0)),
                  pl.BlockSpec((None, 1, F2), lambda j, be: (be[j], 0, 0)),
                  pl.BlockSpec((None, F, D), lambda j, be: (be[j], 0, 0)),
                  pl.BlockSpec((None, 1, D), lambda j, be: (be[j], 0, 0))],
        out_specs=pl.BlockSpec((bm, D), lambda j, be: (j, 0)),
    )
    return pl.pallas_call(
        _expert_kernel,
        out_shape=jax.ShapeDtypeStruct((Mp, D), BF16),
        grid_spec=grid_spec,
        compiler_params=_cp("arbitrary"),
        name="experts",
    )(blk_e, xs, w_up, b_up.reshape(E, 1, F2), w_down, b_down.reshape(E, 1, D))


def _combine_kernel(x_ref, y_ref, g_ref, gf_ref, gfin_ref, o_ref, *, final):
    g = g_ref[...]
    y = g[:, 0:1] * y_ref[0].astype(F32)
    for k in range(1, TOP_K):
        y = y + g[:, k:k + 1] * y_ref[k].astype(F32)
    out = x_ref[...] + gf_ref[...] * y
    if final:
        out = _rms(out, gfin_ref[...])
    o_ref[...] = out


def _combine(x, y4, gate, row0, gate_ffn, group_rows, g_final, final):
    N, D = x.shape
    tm = ROW_TILE
    R = gate_ffn.shape[1]
    tpg = group_rows // tm
    t0 = row0 // tm
    row = lambda w: pl.BlockSpec((tm, w), lambda i: (i, 0))
    return pl.pallas_call(
        functools.partial(_combine_kernel, final=final),
        out_shape=jax.ShapeDtypeStruct((N, D), F32),
        grid=(N // tm,),
        in_specs=[row(D), pl.BlockSpec((TOP_K, tm, D), lambda i: (0, i + t0, 0)),
                  pl.BlockSpec((tm, TOP_K), lambda i: (i + t0, 0)),
                  pl.BlockSpec((None, R, D), lambda i: (i // tpg, 0, 0)), _whole((1, D))],
        out_specs=row(D),
        compiler_params=_cp("parallel"),
        name="combine",
    )(x, y4, gate, gate_ffn, g_final)


class _PageStream:
    def __init__(self, cache_ref, buf_ref, sem_ref, layer, along_lanes):
        self.cache_ref, self.buf_ref, self.sem_ref = cache_ref, buf_ref, sem_ref
        self.layer, self.along_lanes = layer, along_lanes

    def copies(self, pt_ref, b, j, slot, npg):
        out = []
        for i in range(npg):
            win = pl.ds(i * LANES, LANES)
            dst = self.buf_ref.at[slot, :, win] if self.along_lanes else self.buf_ref.at[slot, win, :]
            src = self.cache_ref.at[self.layer, pt_ref[b, j * npg + i]]
            out.append(pltpu.make_async_copy(src, dst, self.sem_ref.at[slot]))
        return out


def _stream_step(streams, pt_ref, nkt, npg):
    s = pl.program_id(0)
    slot = lax.rem(s, 2)

    def each(step, sl, fn):
        for st in streams:
            for c in st.copies(pt_ref, step // nkt, lax.rem(step, nkt), sl, npg):
                fn(c)

    @pl.when(s == 0)
    def _():
        each(s, slot, lambda c: c.start())

    @pl.when(s + 1 < pl.num_programs(0))
    def _():
        each(s + 1, 1 - slot, lambda c: c.start())

    each(s, slot, lambda c: c.wait())
    return s // nkt, lax.rem(s, nkt), slot


def _online_update(s, pv, m_ref, l_ref, acc_ref):
    m_old = m_ref[...]
    m_new = jnp.maximum(m_old, jnp.max(s, axis=-1, keepdims=True))
    m_use = jnp.where(m_new == NEG_INF, 0.0, m_new)
    alpha = jnp.exp(m_old - m_use)
    p = jnp.exp(s - m_use)
    l_ref[...] = alpha * l_ref[...] + jnp.sum(p, axis=-1, keepdims=True)
    acc_ref[...] = alpha * acc_ref[...] + pv(p.astype(BF16))
    m_ref[...] = m_new


def _init_softmax(m_ref, l_ref, acc_ref):
    m_ref[...] = jnp.full(m_ref.shape, NEG_INF, F32)
    l_ref[...] = jnp.zeros(l_ref.shape, F32)
    acc_ref[...] = jnp.zeros(acc_ref.shape, F32)


def _sample_grid(page_table):
    Bd, n_pages = page_table.shape
    npg = min(PAGES_PER_STEP, n_pages)
    nkt = n_pages // npg
    return Bd, npg, nkt, npg * LANES


def _mla_sample_kernel(pt_ref, qa_ref, qp_ref, lat_hbm, kpeT_hbm, latn_ref, kpenT_ref, o_ref,
                       lat_buf, kpe_buf, sem, m_ref, l_ref, acc_ref, *, layer, T, nkt, npg):
    streams = [_PageStream(lat_hbm, lat_buf, sem.at[0], layer, False),
               _PageStream(kpeT_hbm, kpe_buf, sem.at[1], layer, True)]
    b, j, slot = _stream_step(streams, pt_ref, nkt, npg)

    @pl.when(j == 0)
    def _():
        _init_softmax(m_ref, l_ref, acc_ref)

    lat = lat_buf[slot].astype(BF16)
    s = _dot_nt(qa_ref[...], lat) + _dot(qp_ref[...], kpe_buf[slot].astype(BF16))
    _online_update(s, lambda p: _dot(p, lat), m_ref, l_ref, acc_ref)

    @pl.when(j == nkt - 1)
    def _():
        R, C = qa_ref.shape[0], latn_ref.shape[0]
        latn = latn_ref[...].astype(BF16)
        sn = _dot_nt(qa_ref[...], latn) + _dot(qp_ref[...], kpenT_ref[...].astype(BF16))
        t = lax.rem(lax.broadcasted_iota(I32, (R, C), 0), T)
        c = lax.broadcasted_iota(I32, (R, C), 1)
        _online_update(jnp.where(c <= t, sn, NEG_INF), lambda p: _dot(p, latn), m_ref, l_ref, acc_ref)
        o_ref[...] = acc_ref[...] / l_ref[...]


def _mla_sample(page_table, qa, qp, cache_lat, cache_kpeT, layer, latn, kpenT, T):
    Bd, npg, nkt, tk = _sample_grid(page_table)
    _, R, C = qa.shape
    NP = latn.shape[1]
    per_b = lambda r, w: pl.BlockSpec((None, r, w), lambda s, pt: (s // nkt, 0, 0))
    hbm = pl.BlockSpec(memory_space=pl.ANY)
    grid_spec = pltpu.PrefetchScalarGridSpec(
        num_scalar_prefetch=1,
        grid=(Bd * nkt,),
        in_specs=[per_b(R, C), per_b(R, MLA_ROPE), hbm, hbm, per_b(NP, C), per_b(MLA_ROPE, NP)],
        out_specs=per_b(R, C),
        scratch_shapes=[pltpu.VMEM((2, tk, C), F32), pltpu.VMEM((2, MLA_ROPE, tk), F32),
                        pltpu.SemaphoreType.DMA((2, 2)),
                        pltpu.VMEM((R, 1), F32), pltpu.VMEM((R, 1), F32), pltpu.VMEM((R, C), F32)],
    )
    return pl.pallas_call(
        functools.partial(_mla_sample_kernel, layer=layer, T=T, nkt=nkt, npg=npg),
        out_shape=jax.ShapeDtypeStruct((Bd, R, C), F32),
        grid_spec=grid_spec,
        compiler_params=_cp("arbitrary"),
        name="mla_sample",
    )(page_table, qa, qp, cache_lat, cache_kpeT, latn, kpenT)


def _idx_sample_kernel(pt_ref, qi_ref, wi_ref, kiT_hbm, kinT_ref, am_ref, ki_buf, sem, keys_ref,
                       *, layer, T, TP, nkt, npg, n_sel):
    b, j, slot = _stream_step([_PageStream(kiT_hbm, ki_buf, sem.at[0], layer, True)], pt_ref, nkt, npg)
    wi = wi_ref[...]
    tk = npg * LANES

    def keys_of(kiT, valid):
        d = jnp.maximum(_dot(qi_ref[...], kiT), 0.0)
        sc = d[0:TP] * wi[:, 0:1]
        for hh in range(1, IDX_HEADS):
            sc = sc + d[hh * TP:(hh + 1) * TP] * wi[:, hh:hh + 1]
        return jnp.where(valid, _sort_key(sc), KEY_NEG)

    t = lax.broadcasted_iota(I32, (TP, tk), 0)
    keys_ref[:, pl.ds(pl.multiple_of(j * tk, tk), tk)] = keys_of(ki_buf[slot].astype(BF16), t < T)

    @pl.when(j == nkt - 1)
    def _():
        tn = lax.broadcasted_iota(I32, (TP, LANES), 0)
        cn = lax.broadcasted_iota(I32, (TP, LANES), 1)
        keys_ref[:, nkt * tk:] = keys_of(kinT_ref[...].astype(BF16), (tn < T) & (cn <= tn))
        _select_rows(keys_ref, am_ref, n_sel)


def _idx_sample(page_table, qi, wi, cache_kiT, layer, kinT, T, n_sel):
    Bd, npg, nkt, tk = _sample_grid(page_table)
    _, R, _ = qi.shape
    TP = wi.shape[1]
    NP = kinT.shape[2]
    assert NP == LANES
    width = nkt * tk + NP
    per_b = lambda r, w: pl.BlockSpec((None, r, w), lambda s, pt: (s // nkt, 0, 0))
    grid_spec = pltpu.PrefetchScalarGridSpec(
        num_scalar_prefetch=1,
        grid=(Bd * nkt,),
        in_specs=[per_b(R, IDX_DIM), per_b(TP, LANES), pl.BlockSpec(memory_space=pl.ANY), per_b(IDX_DIM, NP)],
        out_specs=per_b(TP, width),
        scratch_shapes=[pltpu.VMEM((2, IDX_DIM, tk), F32), pltpu.SemaphoreType.DMA((1, 2)),
                        pltpu.VMEM((TP, width), I32)],
    )
    return pl.pallas_call(
        functools.partial(_idx_sample_kernel, layer=layer, T=T, TP=TP, nkt=nkt, npg=npg, n_sel=n_sel),
        out_shape=jax.ShapeDtypeStruct((Bd, TP, width), F32),
        grid_spec=grid_spec,
        compiler_params=_cp("arbitrary"),
        name="idx_sample",
    )(page_table, qi, wi, cache_kiT, kinT)


def _dsa_sample_kernel(pt_ref, q_ref, kT_hbm, vT_hbm, knT_ref, vnT_ref, am_ref, amn_ref, b_ref, bn_ref, o_ref,
                       k_buf, v_buf, sem, m_ref, l_ref, acc_ref, *, layer, T, nkt, npg):
    streams = [_PageStream(kT_hbm, k_buf, sem.at[0], layer, True),
               _PageStream(vT_hbm, v_buf, sem.at[1], layer, True)]
    b, j, slot = _stream_step(streams, pt_ref, nkt, npg)
    H = DSA_HEADS

    @pl.when(j == 0)
    def _():
        _init_softmax(m_ref, l_ref, acc_ref)

    def rows_of(am):
        w = am.shape[1]
        return jnp.concatenate([jnp.broadcast_to(am[t:t + 1, :], (H, w)) for t in range(T)], axis=0)

    s = _dot(q_ref[...], k_buf[slot].astype(BF16)) + b_ref[...] + rows_of(am_ref[...])
    _online_update(s, lambda p: _dot_nt(p, v_buf[slot].astype(BF16)), m_ref, l_ref, acc_ref)

    @pl.when(j == nkt - 1)
    def _():
        sn = _dot(q_ref[...], knT_ref[...].astype(BF16)) + bn_ref[...] + rows_of(amn_ref[...])
        _online_update(sn, lambda p: _dot_nt(p, vnT_ref[...].astype(BF16)), m_ref, l_ref, acc_ref)
        o_ref[...] = acc_ref[...] / l_ref[...]


def _dsa_sample(page_table, q, cache_kT, cache_vT, layer, knT, vnT, am, bias, T):
    Bd, npg, nkt, tk = _sample_grid(page_table)
    _, R, C = q.shape
    NP = knT.shape[2]
    TP = am.shape[1]
    last = nkt * tk // NP
    per_b = lambda r, w: pl.BlockSpec((None, r, w), lambda s, pt: (s // nkt, 0, 0))
    hbm = pl.BlockSpec(memory_space=pl.ANY)
    grid_spec = pltpu.PrefetchScalarGridSpec(
        num_scalar_prefetch=1,
        grid=(Bd * nkt,),
        in_specs=[per_b(R, C), hbm, hbm, per_b(C, NP), per_b(C, NP),
                  pl.BlockSpec((None, TP, tk), lambda s, pt: (s // nkt, 0, lax.rem(s, nkt))),
                  pl.BlockSpec((None, TP, NP), lambda s, pt: (s // nkt, 0, last)),
                  pl.BlockSpec((R, tk), lambda s, pt: (0, lax.rem(s, nkt))),
                  pl.BlockSpec((R, NP), lambda s, pt: (0, last))],
        out_specs=per_b(R, C),
        scratch_shapes=[pltpu.VMEM((2, C, tk), F32), pltpu.VMEM((2, C, tk), F32),
                        pltpu.SemaphoreType.DMA((2, 2)),
                        pltpu.VMEM((R, 1), F32), pltpu.VMEM((R, 1), F32), pltpu.VMEM((R, C), F32)],
    )
    return pl.pallas_call(
        functools.partial(_dsa_sample_kernel, layer=layer, T=T, nkt=nkt, npg=npg),
        out_shape=jax.ShapeDtypeStruct((Bd, R, C), F32),
        grid_spec=grid_spec,
        compiler_params=_cp("arbitrary"),
        name="dsa_sample",
    )(page_table, q, cache_kT, cache_vT, knT, vnT, am, am, bias, bias)


def _rope_tables(pos):
    half = MLA_ROPE // 2
    freq = ROPE_THETA ** (-np.arange(half, dtype=np.float64) / half)
    ang = np.asarray(pos, np.float64)[:, None] * freq[None, :]
    cos = np.concatenate([np.cos(ang), np.cos(ang)], axis=1)
    sin = np.concatenate([-np.sin(ang), np.sin(ang)], axis=1)
    cos2 = np.tile(cos, (1, 2 * LANES // MLA_ROPE)).astype(np.float32)
    sin2 = np.tile(sin, (1, 2 * LANES // MLA_ROPE)).astype(np.float32)
    return (jnp.asarray(cos2[:, :LANES]), jnp.asarray(sin2[:, :LANES]),
            jnp.asarray(np.ascontiguousarray(cos2.T)), jnp.asarray(np.ascontiguousarray(sin2.T)))


def _block_diag(w):
    n, c, d = w.shape
    eye = jnp.eye(n, dtype=w.dtype)
    return (eye[:, None, :, None] * w[:, :, None, :]).reshape(n * c, n * d)


def _prep_layer(l, p):
    D = p["w_in"].shape[1]
    w = p["w_in"][l]
    sizes = (MLA_Q_LORA, MLA_KV_LORA, MLA_ROPE, LRU_WIDTH, DSA_HEADS * DSA_HEAD_DIM,
             DSA_KV_HEADS * DSA_HEAD_DIM, DSA_KV_HEADS * DSA_HEAD_DIM, IDX_HEADS * IDX_DIM, IDX_DIM,
             IDX_HEADS, N_BRANCH * D)
    offs = np.cumsum(sizes)[:-1].tolist()
    q_lat, kv_lat, kpe, xr, q_c, k_c, v_c, q_i, k_i, w_i, g = jnp.split(w, offs, axis=1)
    half = MLA_ROPE // 2
    kpe_sw = jnp.concatenate([kpe[:, half:], kpe[:, :half]], axis=1)
    padto = lambda a, n: jnp.pad(a, ((0, 0), (0, n - a.shape[1])))
    w_in = jnp.concatenate([kv_lat, jnp.tile(kpe, (1, ROPE_SLOT)), jnp.tile(kpe_sw, (1, ROPE_SLOT)),
                            xr, k_c, v_c, padto(k_i, LANES), g], axis=1).astype(BF16)
    w_inT = jnp.concatenate([q_lat, kv_lat, q_c, q_i, v_c, padto(w_i, SUBLANES)], axis=1).T.astype(BF16)
    assert w_in.shape[1] == _C_G + N_BRANCH * D and w_inT.shape[0] == _R_END
    w_uq = p["w_uq"][l]
    rope = w_uq[:, :, MLA_NOPE:]
    rope_sw = jnp.concatenate([rope[:, :, half:], rope[:, :, :half]], axis=2)
    row = lambda a: a.reshape(1, -1)
    col = lambda a: jnp.broadcast_to(a.reshape(-1, 1), (a.shape[0], LANES))
    E = p["w_router"].shape[2]
    return {
        "w_in": w_in, "w_inT": w_inT,
        "g_norm_mix": row(p["g_norm_mix"][l]), "g_norm_ffn": row(p["g_norm_ffn"][l]),
        "gqT": col(p["g_q_norm"][l]), "g_kv_norm": row(p["g_kv_norm"][l]), "gkvT": col(p["g_kv_norm"][l]),
        "wnT": jnp.transpose(w_uq[:, :, :MLA_NOPE], (1, 2, 0)).astype(BF16),
        "wrT": rope.reshape(MLA_Q_LORA, -1).T.astype(BF16),
        "wrsT": rope_sw.reshape(MLA_Q_LORA, -1).T.astype(BF16),
        "wukT": jnp.transpose(p["w_uk"][l], (1, 0, 2)).astype(BF16),
        "wuvT": jnp.transpose(p["w_uv"][l], (1, 2, 0)).astype(BF16),
        "bd_full": _block_diag(jnp.transpose(p["w_uv"][l], (1, 0, 2))).astype(BF16),
        "w_conv": p["w_conv"][l], "b_conv": row(p["b_conv"][l]),
        "w_rg_a": _block_diag(p["w_rg_a"][l]).astype(BF16), "b_rg_a": row(p["b_rg_a"][l]),
        "w_rg_i": _block_diag(p["w_rg_i"][l]).astype(BF16), "b_rg_i": row(p["b_rg_i"][l]),
        "lru_lambda": row(p["lru_lambda"][l]),
        "w_br_mla": p["w_br_mla"][l].astype(BF16), "w_br_lru": p["w_br_lru"][l].astype(BF16),
        "w_br_dsa": p["w_br_dsa"][l].astype(BF16), "w_o": p["w_o"][l].astype(BF16),
        "w_router": jnp.pad(p["w_router"][l], ((0, 0), (0, LANES - E))).astype(BF16),
        "b_router": jnp.pad(row(p["b_router"][l]), ((0, 0), (0, LANES - E)), constant_values=NEG_INF),
    }


def _moe(h_all, eidx, w_up, b_up, w_down, b_down):
    N, D = h_all.shape
    E = w_up.shape[0]
    bm = MOE_BLOCK
    M = N * TOP_K
    flat_e = eidx.reshape(M)
    order = jnp.argsort(flat_e)
    e_sorted = flat_e[order]
    counts = jnp.bincount(flat_e, length=E)
    padded = (counts + bm - 1) // bm * bm
    pad_end = jnp.cumsum(padded)
    pad_start = pad_end - padded
    start = jnp.cumsum(counts) - counts
    dest = pad_start[e_sorted] + jnp.arange(M) - start[e_sorted]
    n_blocks = M // bm + E
    blk_e = jnp.minimum(jnp.searchsorted(pad_end, jnp.arange(n_blocks) * bm, side="right"), E - 1).astype(I32)
    row_e = jnp.repeat(blk_e, bm)
    within = jnp.arange(n_blocks * bm) - pad_start[row_e]
    src = jnp.clip(start[row_e] + within, 0, M - 1)
    tok = jnp.where(within < counts[row_e], order[src] // TOP_K, 0).astype(I32)
    pos = dest[jnp.argsort(order)].astype(I32)
    xs = h_all[tok]
    out = _experts(xs, blk_e, w_up, b_up, w_down, b_down)
    return out[pos.reshape(N, TOP_K).T]


def kernel(x_prompt, x_sample, cache_mla_latent, cache_mla_kpe, cache_dsa_k, cache_dsa_v, cache_idx_k, state_lru_h, state_conv, page_table, c_prompt, c_sample, rel_bias, g_norm_mix, g_norm_ffn, w_ada, b_ada, w_in, g_q_norm, w_uq, g_kv_norm, w_uk, w_uv, w_conv, b_conv, w_rg_a, b_rg_a, w_rg_i, b_rg_i, lru_lambda, w_br_mla, w_br_lru, w_br_dsa, w_o, w_router, b_router, w_up, b_up, w_down, b_down, g_final):
    params = dict(g_norm_mix=g_norm_mix, g_norm_ffn=g_norm_ffn, w_in=w_in, g_q_norm=g_q_norm, w_uq=w_uq,
                  g_kv_norm=g_kv_norm, w_uk=w_uk, w_uv=w_uv, w_conv=w_conv, b_conv=b_conv, w_rg_a=w_rg_a,
                  b_rg_a=b_rg_a, w_rg_i=w_rg_i, b_rg_i=b_rg_i, lru_lambda=lru_lambda, w_br_mla=w_br_mla,
                  w_br_lru=w_br_lru, w_br_dsa=w_br_dsa, w_o=w_o, w_router=w_router, b_router=b_router)
    B, S, D = x_prompt.shape
    Bd, T, _ = x_sample.shape
    depth = w_in.shape[0]
    n_pages, page = page_table.shape[1], cache_mla_latent.shape[2]
    past = n_pages * page
    Np, Ns = B * S, Bd * T
    tm = ROW_TILE
    TP = SUBLANES
    NP = LANES
    assert page == LANES and S % max(tm, ATT_TK) == 0 and Ns % tm == 0 and T <= TP
    assert n_pages % min(PAGES_PER_STEP, n_pages) == 0 and tm % T == 0
    H = MLA_HEADS

    rope_p = _rope_tables(np.arange(S))
    rope_s = _rope_tables(np.tile(past + np.arange(T), tm // T))
    ntypes = -(-(_BUCKET_FAR + ATT_TK - 1) // ATT_TQ) + 1
    bias_p = _bias_tiles(rel_bias, ntypes, ATT_TK, ATT_TQ, 0, ATT_TQ, -1, 1, "bias_prompt")
    bias_s = _bias_tiles(rel_bias, T, SUBLANES, past + NP, past, 1, 0, -1, "bias_sample")
    bias_s = bias_s[:, :, 0, :].reshape(T * DSA_HEADS, past + NP)
    n_sel_s = min(DSA_TOPK_MAX, (past + T) // 4)
    feat_major = lambda c: jnp.swapaxes(c.reshape(c.shape[:3] + (-1,)), 2, 3)
    cache_kpeT, cache_kiT = feat_major(cache_mla_kpe), feat_major(cache_idx_k)
    cache_kT, cache_vT = feat_major(cache_dsa_k), feat_major(cache_dsa_v)

    xp = x_prompt.reshape(Np, D)
    xs = x_sample.reshape(Ns, D)
    c_all = jnp.concatenate([c_prompt, c_sample], axis=0)
    new_p = [[] for _ in range(7)]
    new_s = [[] for _ in range(7)]
    per_tok = lambda a: jnp.repeat(a, T, axis=0).reshape(Ns // tm, tm, D)
    for l in range(depth):
        lw = _prep_layer(l, params)
        ada = _matmul(c_all, w_ada[l], b_ada[l].reshape(1, -1), D, "ada")
        ada_p = [a.reshape(B, 1, D) for a in jnp.split(ada[:B], 6, axis=1)]
        ada_s = [per_tok(a) for a in jnp.split(ada[B:], 6, axis=1)]

        (qT, ckv, kp, ckvT, kpe, xr, qcT, kc, vc, kcb, vT, qiT, ki, kib, wiT, gates) = _in_proj(
            xp, ada_p[0], ada_p[1], S, rope_p, lw)
        o_mla = _mla_prompt(qT, kp, ckvT, lw["wuvT"], B, S)
        o_lru, hT = _rglru_prompt(xr, lw, B, S)
        o_dsa = _dsa_prompt(qcT, qiT, wiT, kcb, vT, kib, bias_p, B, S)
        xp = _merge(xp, o_mla, o_lru, o_dsa, gates, ada_p[2], S, lw)
        xr3 = xr.reshape(B, S, LRU_WIDTH)
        conv_p = jnp.concatenate([jnp.zeros((B, CONV_W - 1, LRU_WIDTH), F32), xr3], axis=1)[:, S:]
        for i, a in enumerate((ckv.reshape(B, S, -1), kpe[:, :MLA_ROPE].reshape(B, S, -1),
                               kc.reshape(B, S, DSA_KV_HEADS, DSA_HEAD_DIM),
                               vc.reshape(B, S, DSA_KV_HEADS, DSA_HEAD_DIM),
                               ki[:, :IDX_DIM].reshape(B, S, -1), hT.reshape(B, -1), conv_p)):
            new_p[i].append(a)

        (qT, ckv, kp, ckvT, kpe, xr, qcT, kc, vc, kcb, vT, qiT, ki, kib, wiT, gates) = _in_proj(
            xs, ada_s[0], ada_s[1], tm, rope_s, lw)
        q4 = jnp.transpose(qT.reshape(QK_W, Ns // LANES, H, LANES), (2, 1, 3, 0)).reshape(H, Bd, T, QK_W)
        q4 = jnp.transpose(q4, (1, 0, 2, 3))
        qa = q4[..., :MLA_KV_LORA].reshape(Bd, H * T, MLA_KV_LORA)
        qp = jnp.stack([q4[:, h, :, MLA_KV_LORA + (h % ROPE_SLOT) * MLA_ROPE:
                           MLA_KV_LORA + (h % ROPE_SLOT + 1) * MLA_ROPE] for h in range(H)], axis=1)
        qp = qp.reshape(Bd, H * T, MLA_ROPE)
        padk = lambda a: jnp.pad(a.reshape(Bd, T, -1), ((0, 0), (0, NP - T), (0, 0)))
        padkT = lambda a: jnp.swapaxes(padk(a), 1, 2)
        o_lat = _mla_sample(page_table, qa, qp, cache_mla_latent, cache_kpeT, l,
                            padk(ckv), padkT(kpe[:, :MLA_ROPE]), T)
        o_lat = jnp.transpose(o_lat.reshape(Bd, H, T, MLA_KV_LORA), (0, 2, 1, 3)).reshape(Ns, H * MLA_KV_LORA)
        o_mla = _matmul(o_lat, lw["bd_full"], jnp.zeros((1, H * MLA_V), F32), H * MLA_V,
                        "mla_sample_out").astype(BF16)

        xr_t = jnp.transpose(xr.reshape(Bd, T, LRU_WIDTH), (1, 0, 2))
        o_lru_t, hT = _rglru_sample(xr_t, jnp.transpose(state_conv[l], (1, 0, 2)), state_lru_h[l], lw)
        o_lru = jnp.transpose(o_lru_t, (1, 0, 2)).reshape(Ns, LRU_WIDTH)
        conv_s = jnp.concatenate([state_conv[l], xr.reshape(Bd, T, LRU_WIDTH)], axis=1)[:, T:]

        qi_s = jnp.transpose(qiT.reshape(IDX_HEADS, IDX_DIM, Bd, T), (2, 0, 3, 1))
        qi_s = jnp.pad(qi_s, ((0, 0), (0, 0), (0, TP - T), (0, 0))).reshape(Bd, IDX_HEADS * TP, IDX_DIM)
        wi_s = jnp.transpose(wiT.reshape(SUBLANES, Bd, T), (1, 2, 0))
        wi_s = jnp.pad(wi_s, ((0, 0), (0, TP - T), (0, LANES - SUBLANES)))
        am = _idx_sample(page_table, qi_s, wi_s, cache_kiT, l, padkT(ki[:, :IDX_DIM]), T, n_sel_s)
        qc4 = jnp.transpose(qcT.reshape(DSA_KV_HEADS, DSA_GROUP, DSA_HEAD_DIM, Bd, T), (3, 4, 0, 1, 2))
        zeros = jnp.zeros_like(qc4)
        q_s = jnp.stack([jnp.concatenate([qc4[:, :, 0], zeros[:, :, 0]], axis=-1),
                         jnp.concatenate([zeros[:, :, 1], qc4[:, :, 1]], axis=-1)], axis=2)
        q_s = q_s.reshape(Bd, T * DSA_HEADS, DSA_KV_HEADS * DSA_HEAD_DIM)
        o_s = _dsa_sample(page_table, q_s, cache_kT, cache_vT, l, padkT(kc), padkT(vc), am, bias_s, T)
        o_s = o_s.reshape(Bd, T, DSA_KV_HEADS, DSA_GROUP, DSA_KV_HEADS, DSA_HEAD_DIM)
        o_dsa = jnp.stack([o_s[:, :, g, :, g] for g in range(DSA_KV_HEADS)], axis=2)
        o_dsa = o_dsa.reshape(Ns, DSA_HEADS * DSA_HEAD_DIM).astype(BF16)
        xs = _merge(xs, o_mla, o_lru, o_dsa, gates, ada_s[2], tm, lw)
        for i, a in enumerate((ckv.reshape(Bd, T, -1), kpe[:, :MLA_ROPE].reshape(Bd, T, -1),
                               kc.reshape(Bd, T, DSA_KV_HEADS, DSA_HEAD_DIM),
                               vc.reshape(Bd, T, DSA_KV_HEADS, DSA_HEAD_DIM),
                               ki[:, :IDX_DIM].reshape(Bd, T, -1), hT, conv_s)):
            new_s[i].append(a)

        h_p, e_p, g_p = _router(xp, ada_p[3], ada_p[4], S, lw)
        h_s, e_s, g_s = _router(xs, ada_s[3], ada_s[4], tm, lw)
        gate = jnp.concatenate([g_p, g_s], axis=0)
        y4 = _moe(jnp.concatenate([h_p, h_s], axis=0), jnp.concatenate([e_p, e_s], axis=0),
                  w_up[l], b_up[l], w_down[l], b_down[l])
        final = l == depth - 1
        gfin = g_final.reshape(1, D)
        xp = _combine(xp, y4, gate, 0, ada_p[5], S, gfin, final)
        xs = _combine(xs, y4, gate, Np, ada_s[5], tm, gfin, final)

    outs = [xp.reshape(B, S, D), xs.reshape(Bd, T, D)]
    outs += [jnp.stack(a) for a in new_p] + [jnp.stack(a) for a in new_s]
    return tuple(outs)
```

```python
import functools
import math

import numpy as np
import jax
import jax.numpy as jnp
from jax import lax
from jax.experimental import pallas as pl
from jax.experimental.pallas import tpu as pltpu

F32 = jnp.float32
BF16 = jnp.bfloat16
I32 = jnp.int32

EPS = 1e-6
MLA_HEADS = 8
MLA_NOPE = 64
MLA_ROPE = 32
MLA_V = 64
MLA_Q_LORA = 384
MLA_KV_LORA = 256
MLA_SCALE = (MLA_NOPE + MLA_ROPE) ** -0.5
ROPE_THETA = 10000.0
LRU_WIDTH = 512
LRU_BLOCKS = 8
CONV_W = 4
LRU_C = 8.0
DSA_HEADS = 8
DSA_KV_HEADS = 2
DSA_GROUP = DSA_HEADS // DSA_KV_HEADS
DSA_HEAD_DIM = 64
DSA_SCALE = DSA_HEAD_DIM ** -0.5
IDX_HEADS = 4
IDX_DIM = 64
IDX_SCALE = IDX_DIM ** -0.5
DSA_TOPK_MAX = 256
N_BUCKETS = 32
MAX_DISTANCE = 128
N_EXPERTS = 32
TOP_K = 4
SWIGLU_ALPHA = 1.702
SWIGLU_LIMIT = 7.0
N_BRANCH = 3

LANES = 128
SUBLANES = 8
ROPE_SLOT = LANES // MLA_ROPE
QK_W = MLA_KV_LORA + LANES
VMEM_LIMIT = 56 * 1024 * 1024
INT_MIN = -2 ** 31
KEY_NEG = int(np.array(-np.inf, np.float32).view(np.int32)) ^ 0x7FFFFFFF
NEG_INF = float("-inf")

ROW_TILE = 256
ATT_TQ = LANES
ATT_TK = 256
MLA_TK = 512
MOE_BLOCK = 256
PAGES_PER_STEP = 32
COUNT_CHAINS = 4


def _cp(*sem):
    return pltpu.CompilerParams(dimension_semantics=sem, vmem_limit_bytes=VMEM_LIMIT)


def _whole(shape):
    nd = len(shape)
    return pl.BlockSpec(shape, lambda *a: (0,) * nd)


def _dot(a, b):
    return jnp.dot(a, b, preferred_element_type=F32)


def _dot_nt(a, b):
    return lax.dot_general(a, b, (((1,), (1,)), ((), ())), preferred_element_type=F32)


def _rms(x, g):
    return x * lax.rsqrt(jnp.mean(x * x, axis=-1, keepdims=True) + EPS) * g


def _rms_t(xT, gT):
    return xT * lax.rsqrt(jnp.mean(xT * xT, axis=0, keepdims=True) + EPS) * gT


def _lane_tile(x, n):
    return x if n == 1 else jnp.concatenate([x] * n, axis=1)


def _mm_kernel(x_ref, w_ref, b_ref, o_ref):
    o_ref[...] = _dot(x_ref[...].astype(BF16), w_ref[...].astype(BF16)) + b_ref[...]


def _matmul(x, w, b, tn, name):
    M, K = x.shape
    N = w.shape[1]
    return pl.pallas_call(
        _mm_kernel,
        out_shape=jax.ShapeDtypeStruct((M, N), F32),
        grid=(N // tn,),
        in_specs=[pl.BlockSpec((M, K), lambda j: (0, 0)),
                  pl.BlockSpec((K, tn), lambda j: (0, j)),
                  pl.BlockSpec((1, tn), lambda j: (0, j))],
        out_specs=pl.BlockSpec((M, tn), lambda j: (0, j)),
        compiler_params=_cp("parallel"),
        name=name,
    )(x, w, b)


def _bucket_starts():
    n = np.arange(0, 8 * MAX_DISTANCE)
    max_exact = N_BUCKETS // 2

    def buckets(dt):
        nf = np.maximum(n, 1).astype(dt)
        large = max_exact + (np.log(nf / dt(max_exact)) / dt(math.log(MAX_DISTANCE / max_exact))
                             * dt(N_BUCKETS - max_exact)).astype(np.int32)
        return np.where(n < max_exact, n, np.minimum(large, N_BUCKETS - 1))

    b64 = buckets(np.float64)
    assert (buckets(np.float32) == b64).all() and (np.diff(b64) >= 0).all()
    starts = [int(np.argmax(b64 >= k)) if (b64 >= k).any() else int(n[-1]) + 1 for k in range(N_BUCKETS)]
    far = int(np.argmax(b64 == N_BUCKETS - 1))
    assert (b64[far:] == N_BUCKETS - 1).all()
    return starts, far


_BUCKET_STARTS, _BUCKET_FAR = _bucket_starts()


def _bias_kernel(tab_ref, o_ref, *, delta0, delta_step, row_mult, col_mult):
    i = pl.program_id(0)
    R, C = o_ref.shape[2:]
    r = lax.broadcasted_iota(I32, (R, C), 0)
    c = lax.broadcasted_iota(I32, (R, C), 1)
    dist = delta0 + delta_step * i + row_mult * r + col_mult * c
    for h in range(DSA_HEADS):
        val = jnp.full((R, C), tab_ref[0, h], F32)
        for k in range(1, N_BUCKETS):
            val = jnp.where(dist >= _BUCKET_STARTS[k], tab_ref[k, h], val)
        o_ref[0, h] = val


def _bias_tiles(rel_bias, n, R, C, delta0, delta_step, row_mult, col_mult, name):
    return pl.pallas_call(
        functools.partial(_bias_kernel, delta0=delta0, delta_step=delta_step, row_mult=row_mult,
                          col_mult=col_mult),
        out_shape=jax.ShapeDtypeStruct((n, DSA_HEADS, R, C), F32),
        grid=(n,),
        in_specs=[pl.BlockSpec(memory_space=pltpu.SMEM)],
        out_specs=pl.BlockSpec((1, DSA_HEADS, R, C), lambda i: (i, 0, 0, 0)),
        compiler_params=_cp("parallel"),
        name=name,
    )(rel_bias)


_C_KV = 0
_C_KPE = _C_KV + MLA_KV_LORA
_C_KPES = _C_KPE + LANES
_C_XR = _C_KPES + LANES
_C_KC = _C_XR + LRU_WIDTH
_C_VC = _C_KC + LANES
_C_KI = _C_VC + LANES
_C_G = _C_KI + LANES
_R_QLAT = 0
_R_KV = _R_QLAT + MLA_Q_LORA
_R_QC = _R_KV + MLA_KV_LORA
_R_QI = _R_QC + DSA_HEADS * DSA_HEAD_DIM
_R_VC = _R_QI + IDX_HEADS * IDX_DIM
_R_WI = _R_VC + LANES
_R_END = _R_WI + SUBLANES


def _inproj_kernel(x_ref, gn_ref, sh_ref, sc_ref, cos_ref, sin_ref, cosT_ref, sinT_ref, w_ref, wT_ref,
                   gqT_ref, wnT_ref, wukT_ref, wrT_ref, wrsT_ref, gkv_ref, gkvT_ref,
                   qT_out, ckv_out, kp_out, ckvT_out, kpe_out, xr_out, qcT_out, kc_out, vc_out, kcb_out,
                   vT_out, qiT_out, ki_out, kib_out, wiT_out, gates_out):
    tm, D = x_ref.shape
    nl = tm // LANES
    hb = (_rms(x_ref[...], gn_ref[...]) * (1.0 + sc_ref[...]) + sh_ref[...]).astype(BF16)

    def seg(lo, width):
        return _dot(hb, w_ref[:, lo:lo + width])

    def seg_t(lo, rows):
        return _dot_nt(wT_ref[lo:lo + rows, :], hb)

    cqT = _rms_t(seg_t(_R_QLAT, MLA_Q_LORA), _lane_tile(gqT_ref[...], nl)).astype(BF16)
    qpeT = (_dot(wrT_ref[...], cqT) * cosT_ref[...] + _dot(wrsT_ref[...], cqT) * sinT_ref[...]) * MLA_SCALE
    sub = lax.broadcasted_iota(I32, (LANES, tm), 0)
    for h in range(MLA_HEADS):
        qnT = _dot(wnT_ref[h], cqT).astype(BF16)
        qaT = (_dot(wukT_ref[h], qnT) * MLA_SCALE).astype(BF16)
        blk = qpeT[(h // ROPE_SLOT) * LANES:(h // ROPE_SLOT + 1) * LANES, :]
        slot = h % ROPE_SLOT
        keep = (sub >= slot * MLA_ROPE) & (sub < (slot + 1) * MLA_ROPE)
        peT = jnp.where(keep, blk, 0.0).astype(BF16)
        for t in range(nl):
            c0 = (t * MLA_HEADS + h) * LANES
            qT_out[0:MLA_KV_LORA, c0:c0 + LANES] = qaT[:, t * LANES:(t + 1) * LANES]
            qT_out[MLA_KV_LORA:QK_W, c0:c0 + LANES] = peT[:, t * LANES:(t + 1) * LANES]

    ckv = _rms(seg(_C_KV, MLA_KV_LORA), gkv_ref[...])
    ckv_out[...] = ckv
    cos = cos_ref[...]
    sin = sin_ref[...]
    kpe = seg(_C_KPE, LANES) * cos + seg(_C_KPES, LANES) * sin
    kpe_out[...] = kpe
    kp_out[:, 0:MLA_KV_LORA] = ckv.astype(BF16)
    kp_out[:, MLA_KV_LORA:QK_W] = kpe.astype(BF16)
    ckvT_out[...] = _rms_t(seg_t(_R_KV, MLA_KV_LORA), _lane_tile(gkvT_ref[...], nl)).astype(BF16)

    xr_out[...] = seg(_C_XR, LRU_WIDTH)

    qcT_out[...] = (seg_t(_R_QC, DSA_HEADS * DSA_HEAD_DIM) * DSA_SCALE).astype(BF16)
    kc = seg(_C_KC, LANES)
    kc_out[...] = kc
    kcb_out[...] = kc.astype(BF16)
    vc_out[...] = seg(_C_VC, LANES)
    vT_out[...] = seg_t(_R_VC, LANES).astype(BF16)
    qiT_out[...] = (seg_t(_R_QI, IDX_HEADS * IDX_DIM) * IDX_SCALE).astype(BF16)
    ki = seg(_C_KI, LANES)
    ki_out[...] = ki
    kib_out[...] = ki.astype(BF16)
    wiT_out[...] = seg_t(_R_WI, SUBLANES) * (IDX_HEADS ** -0.5)

    for j in range(N_BRANCH):
        gates_out[:, j * D:(j + 1) * D] = jax.nn.sigmoid(seg(_C_G + j * D, D)).astype(BF16)


def _in_proj(x, shift, scale, group_rows, rope, lw):
    N, D = x.shape
    tm = ROW_TILE
    R = shift.shape[1]
    tiles_per_group = group_rows // tm
    cos, sin, cosT, sinT = rope
    ptiles = cos.shape[0] // tm
    row = lambda w: pl.BlockSpec((tm, w), lambda i: (i, 0))
    col = lambda r: pl.BlockSpec((r, tm), lambda i: (0, i))
    mod = pl.BlockSpec((None, R, D), lambda i: (i // tiles_per_group, 0, 0))
    outs = [
        (jax.ShapeDtypeStruct((QK_W, N * MLA_HEADS), BF16),
         pl.BlockSpec((QK_W, tm * MLA_HEADS), lambda i: (0, i))),
        (jax.ShapeDtypeStruct((N, MLA_KV_LORA), F32), row(MLA_KV_LORA)),
        (jax.ShapeDtypeStruct((N, QK_W), BF16), row(QK_W)),
        (jax.ShapeDtypeStruct((MLA_KV_LORA, N), BF16), col(MLA_KV_LORA)),
        (jax.ShapeDtypeStruct((N, LANES), F32), row(LANES)),
        (jax.ShapeDtypeStruct((N, LRU_WIDTH), F32), row(LRU_WIDTH)),
        (jax.ShapeDtypeStruct((512, N), BF16), col(512)),
        (jax.ShapeDtypeStruct((N, LANES), F32), row(LANES)),
        (jax.ShapeDtypeStruct((N, LANES), F32), row(LANES)),
        (jax.ShapeDtypeStruct((N, LANES), BF16), row(LANES)),
        (jax.ShapeDtypeStruct((LANES, N), BF16), col(LANES)),
        (jax.ShapeDtypeStruct((256, N), BF16), col(256)),
        (jax.ShapeDtypeStruct((N, LANES), F32), row(LANES)),
        (jax.ShapeDtypeStruct((N, LANES), BF16), row(LANES)),
        (jax.ShapeDtypeStruct((SUBLANES, N), F32), col(SUBLANES)),
        (jax.ShapeDtypeStruct((N, N_BRANCH * D), BF16), row(N_BRANCH * D)),
    ]
    return pl.pallas_call(
        _inproj_kernel,
        out_shape=[o[0] for o in outs],
        grid=(N // tm,),
        in_specs=[row(D), _whole((1, D)), mod, mod,
                  pl.BlockSpec((tm, LANES), lambda i: (i % ptiles, 0)),
                  pl.BlockSpec((tm, LANES), lambda i: (i % ptiles, 0)),
                  pl.BlockSpec((2 * LANES, tm), lambda i: (0, i % ptiles)),
                  pl.BlockSpec((2 * LANES, tm), lambda i: (0, i % ptiles)),
                  _whole(lw["w_in"].shape), _whole(lw["w_inT"].shape), _whole(lw["gqT"].shape),
                  _whole(lw["wnT"].shape), _whole(lw["wukT"].shape), _whole(lw["wrT"].shape),
                  _whole(lw["wrsT"].shape), _whole((1, MLA_KV_LORA)), _whole(lw["gkvT"].shape)],
        out_specs=[o[1] for o in outs],
        compiler_params=_cp("parallel"),
        name="in_proj",
    )(x, lw["g_norm_mix"], shift, scale, cos, sin, cosT, sinT, lw["w_in"], lw["w_inT"], lw["gqT"],
      lw["wnT"], lw["wukT"], lw["wrT"], lw["wrsT"], lw["g_kv_norm"], lw["gkvT"])


def _mla_prompt_kernel(qi_tab, ki_tab, q_ref, k_ref, vT_ref, wuvT_ref, o_ref, m_ref, l_ref, acc_ref,
                       *, tq, tk):
    step = pl.program_id(1)
    qi = qi_tab[step]
    ki = ki_tab[step]
    last = (qi * tq + tq - 1) // tk
    H = MLA_HEADS

    @pl.when(ki == 0)
    def _():
        m_ref[...] = jnp.full(m_ref.shape, NEG_INF, F32)
        l_ref[...] = jnp.zeros(l_ref.shape, F32)
        acc_ref[...] = jnp.zeros(acc_ref.shape, F32)

    def update(masked):
        s = _dot(k_ref[...], q_ref[...])
        if masked:
            kpos = ki * tk + lax.broadcasted_iota(I32, (tk, H * tq), 0)
            qpos = qi * tq + (lax.broadcasted_iota(I32, (tk, H * tq), 1) & (tq - 1))
            s = jnp.where(kpos <= qpos, s, NEG_INF)
        m_old = m_ref[...]
        m_new = jnp.maximum(m_old, jnp.max(s, axis=0, keepdims=True))
        alpha = jnp.exp(m_old - m_new)
        p = jnp.exp(s - m_new)
        l_ref[...] = alpha * l_ref[...] + jnp.sum(p, axis=0, keepdims=True)
        acc_ref[...] = alpha * acc_ref[...] + _dot(vT_ref[...], p.astype(BF16))
        m_ref[...] = m_new

    @pl.when(ki < last)
    def _():
        update(False)

    @pl.when(ki == last)
    def _():
        update(True)
        oT = (acc_ref[...] / l_ref[...]).astype(BF16)
        heads = [_dot(wuvT_ref[h], oT[:, h * tq:(h + 1) * tq]) for h in range(H)]
        o_ref[...] = jnp.concatenate(heads, axis=0).T.astype(o_ref.dtype)


def _mla_prompt(qT, kp, ckvT, wuvT, B, S):
    tq, tk = ATT_TQ, MLA_TK
    assert tq & (tq - 1) == 0
    nq = S // tq
    pairs = [(qi, ki) for qi in range(nq) for ki in range((qi * tq + tq - 1) // tk + 1)]
    qi_tab = jnp.asarray(np.array([p[0] for p in pairs], np.int32))
    ki_tab = jnp.asarray(np.array([p[1] for p in pairs], np.int32))
    H = MLA_HEADS
    grid_spec = pltpu.PrefetchScalarGridSpec(
        num_scalar_prefetch=2,
        grid=(B, len(pairs)),
        in_specs=[pl.BlockSpec((QK_W, H * tq), lambda b, s, qt, kt: (0, b * nq + qt[s])),
                  pl.BlockSpec((tk, QK_W), lambda b, s, qt, kt: (b * (S // tk) + kt[s], 0)),
                  pl.BlockSpec((MLA_KV_LORA, tk), lambda b, s, qt, kt: (0, b * (S // tk) + kt[s])),
                  pl.BlockSpec(wuvT.shape, lambda b, s, qt, kt: (0, 0, 0))],
        out_specs=pl.BlockSpec((tq, H * MLA_V), lambda b, s, qt, kt: (b * nq + qt[s], 0)),
        scratch_shapes=[pltpu.VMEM((1, H * tq), F32), pltpu.VMEM((1, H * tq), F32),
                        pltpu.VMEM((MLA_KV_LORA, H * tq), F32)],
    )
    return pl.pallas_call(
        functools.partial(_mla_prompt_kernel, tq=tq, tk=tk),
        out_shape=jax.ShapeDtypeStruct((B * S, H * MLA_V), BF16),
        grid_spec=grid_spec,
        compiler_params=_cp("parallel", "arbitrary"),
        name="mla_prompt",
    )(qi_tab, ki_tab, qT, kp, ckvT, wuvT)


def _sort_key(score):
    bits = lax.bitcast_convert_type(score, I32)
    key = jnp.where(bits < 0, bits ^ 0x7FFFFFFF, bits)
    return jnp.where(score == 0.0, 0, key)


def _radix_threshold(count_ge, shape, n_sel):
    def bit_body(i, w):
        candw = w | lax.shift_left(jnp.int32(1), 31 - i)
        return jnp.where(count_ge(candw ^ INT_MIN) >= n_sel, candw, w)
    return lax.fori_loop(0, 32, bit_body, jnp.zeros(shape, I32)) ^ INT_MIN


def _select_rows(keys_ref, am_ref, n_sel):
    rows, width = keys_ref.shape
    nch = width // LANES

    def count(pred):
        parts = [jnp.zeros((rows, LANES), F32) for _ in range(min(COUNT_CHAINS, nch))]
        for c in range(nch):
            hit = jnp.where(pred(keys_ref[:, c * LANES:(c + 1) * LANES]), 1.0, 0.0)
            parts[c % len(parts)] = parts[c % len(parts)] + hit
        cnt = parts[0]
        for part in parts[1:]:
            cnt = cnt + part
        return jnp.sum(cnt, axis=-1, keepdims=True)

    bc = lambda t: jnp.broadcast_to(t, (rows, LANES))
    thr = _radix_threshold(lambda t: count(lambda k: k >= bc(t)), (rows, 1), n_sel)
    thr_b = bc(thr)
    n_gt = count(lambda k: k > thr_b)
    n_ge = count(lambda k: k >= thr_b)
    tied = jnp.max(jnp.where((n_ge > n_sel) & (thr != KEY_NEG), 1.0, 0.0)) > 0.0

    @pl.when(jnp.logical_not(tied))
    def _():
        for c in range(nch):
            k = keys_ref[:, c * LANES:(c + 1) * LANES]
            am_ref[:, c * LANES:(c + 1) * LANES] = jnp.where((k >= thr_b) & (k != KEY_NEG), 0.0, NEG_INF)

    @pl.when(tied)
    def _():
        need = bc(n_sel - n_gt)
        jj = lax.broadcasted_iota(I32, (LANES, LANES), 0)
        cc = lax.broadcasted_iota(I32, (LANES, LANES), 1)
        before = jnp.where(jj < cc, 1.0, 0.0).astype(BF16)
        ones = jnp.ones((LANES, LANES), BF16)

        def body(c, base):
            sl = pl.ds(pl.multiple_of(c * LANES, LANES), LANES)
            k = keys_ref[:, sl]
            eq = k == thr_b
            eqb = jnp.where(eq, 1.0, 0.0).astype(BF16)
            rank = base + _dot(eqb, before)
            sel = ((k > thr_b) | (eq & (rank < need))) & (k != KEY_NEG)
            am_ref[:, sl] = jnp.where(sel, 0.0, NEG_INF)
            return base + _dot(eqb, ones)
        lax.fori_loop(0, nch, body, jnp.zeros((rows, LANES), F32))


def _select_cols(keys_ref, am_ref, nblk, blk, n_sel):
    Q = keys_ref.shape[1]
    acc_rows = COUNT_CHAINS * SUBLANES
    fold = lambda x: jnp.sum(x.reshape(blk // acc_rows, acc_rows, Q), axis=0)

    def count(pred):
        def body(b, cnt):
            k = keys_ref[pl.ds(pl.multiple_of(b * blk, blk), blk), :]
            return cnt + fold(jnp.where(pred(k), 1.0, 0.0))
        cnt = lax.fori_loop(0, nblk, body, jnp.zeros((acc_rows, Q), F32))
        return jnp.sum(cnt, axis=0, keepdims=True)

    bc = lambda t: jnp.broadcast_to(t, (blk, Q))
    thr = _radix_threshold(lambda t: count(lambda k: k >= bc(t)), (1, Q), n_sel)
    thr_b = bc(thr)
    n_gt = count(lambda k: k > thr_b)
    n_ge = count(lambda k: k >= thr_b)
    tied = jnp.max(jnp.where((n_ge > n_sel) & (thr != KEY_NEG), 1.0, 0.0)) > 0.0

    @pl.when(jnp.logical_not(tied))
    def _():
        def body(b, carry):
            sl = pl.ds(pl.multiple_of(b * blk, blk), blk)
            k = keys_ref[sl, :]
            am_ref[sl, :] = jnp.where((k >= thr_b) & (k != KEY_NEG), 0.0, NEG_INF)
            return carry
        lax.fori_loop(0, nblk, body, 0)

    @pl.when(tied)
    def _():
        need = bc(n_sel - n_gt)
        rr = lax.broadcasted_iota(I32, (blk, blk), 0)
        jj = lax.broadcasted_iota(I32, (blk, blk), 1)
        before = jnp.where(jj < rr, 1.0, 0.0).astype(BF16)

        def body(b, base):
            sl = pl.ds(pl.multiple_of(b * blk, blk), blk)
            k = keys_ref[sl, :]
            eq = k == thr_b
            eqf = jnp.where(eq, 1.0, 0.0)
            rank = base + _dot(before, eqf.astype(BF16))
            sel = ((k > thr_b) | (eq & (rank < need))) & (k != KEY_NEG)
            am_ref[sl, :] = jnp.where(sel, 0.0, NEG_INF)
            return base + jnp.sum(eqf, axis=0, keepdims=True)
        lax.fori_loop(0, nblk, body, jnp.zeros((1, Q), F32))


def _dsa_prompt_kernel(qcT_ref, qiT_ref, wiT_ref, kc_ref, vT_ref, ki_ref, bias_ref, o_ref,
                       rq_ref, keys_ref, am_ref, m_ref, l_ref, acc_ref, *, tq, tk, n_sel):
    qidx = pl.program_id(1)
    nkb = (qidx * tq + tq - 1) // tk + 1
    H, G, Dh = DSA_HEADS, DSA_GROUP, DSA_HEAD_DIM

    zeros = jnp.zeros((Dh, tq), BF16)
    for h in range(H):
        piece = qcT_ref[h * Dh:(h + 1) * Dh, :]
        rq_ref[:, h * tq:(h + 1) * tq] = jnp.concatenate(
            [piece, zeros] if h // G == 0 else [zeros, piece], axis=0)
    ri = jnp.concatenate([jnp.concatenate([qiT_ref[hh * IDX_DIM:(hh + 1) * IDX_DIM, :], zeros], axis=0)
                          for hh in range(IDX_HEADS)], axis=1)
    wiT = wiT_ref[...]

    krow = lax.broadcasted_iota(I32, (tk, tq), 0)
    qpos = qidx * tq + lax.broadcasted_iota(I32, (tk, tq), 1)

    def p1(kb, carry):
        ks = pl.multiple_of(kb * tk, tk)
        d = jnp.maximum(_dot(ki_ref[pl.ds(ks, tk), :], ri), 0.0)
        sc = d[:, 0:tq] * wiT[0:1, :]
        for hh in range(1, IDX_HEADS):
            sc = sc + d[:, hh * tq:(hh + 1) * tq] * wiT[hh:hh + 1, :]
        keys_ref[pl.ds(ks, tk), :] = jnp.where(kb * tk + krow <= qpos, _sort_key(sc), KEY_NEG)
        return carry
    lax.fori_loop(0, nkb, p1, 0)

    _select_cols(keys_ref, am_ref, nkb, tk, n_sel)

    m_ref[...] = jnp.full(m_ref.shape, NEG_INF, F32)
    l_ref[...] = jnp.zeros(l_ref.shape, F32)
    acc_ref[...] = jnp.zeros(acc_ref.shape, F32)

    def p3(kb, carry):
        ks = pl.multiple_of(kb * tk, tk)
        ty = jnp.minimum((qidx * tq - kb * tk) // tq, bias_ref.shape[0] - 1)
        bias = jnp.concatenate([bias_ref[ty, h] for h in range(H)], axis=1)
        s = _dot(kc_ref[pl.ds(ks, tk), :], rq_ref[...]) + bias + _lane_tile(am_ref[pl.ds(ks, tk), :], H)
        m_old = m_ref[...]
        m_new = jnp.maximum(m_old, jnp.max(s, axis=0, keepdims=True))
        m_use = jnp.where(m_new == NEG_INF, 0.0, m_new)
        alpha = jnp.exp(m_old - m_use)
        p = jnp.exp(s - m_use)
        l_ref[...] = alpha * l_ref[...] + jnp.sum(p, axis=0, keepdims=True)
        acc_ref[...] = alpha * acc_ref[...] + _dot(vT_ref[:, pl.ds(ks, tk)], p.astype(BF16))
        m_ref[...] = m_new
        return carry
    lax.fori_loop(0, nkb, p3, 0)

    oT = acc_ref[...] / l_ref[...]
    heads = [oT[(h // G) * Dh:(h // G + 1) * Dh, h * tq:(h + 1) * tq] for h in range(H)]
    o_ref[...] = jnp.concatenate(heads, axis=0).T.astype(o_ref.dtype)


def _dsa_prompt(qcT, qiT, wiT, kcb, vT, kib, biasT, B, S):
    tq, tk = ATT_TQ, ATT_TK
    nq = S // tq
    n_sel = min(DSA_TOPK_MAX, S // 4)
    H = DSA_HEADS
    qcol = lambda r: pl.BlockSpec((r, tq), lambda b, i: (0, b * nq + i))
    seq = pl.BlockSpec((S, LANES), lambda b, i: (b, 0))
    return pl.pallas_call(
        functools.partial(_dsa_prompt_kernel, tq=tq, tk=tk, n_sel=n_sel),
        out_shape=jax.ShapeDtypeStruct((B * S, H * DSA_HEAD_DIM), BF16),
        grid=(B, nq),
        in_specs=[qcol(512), qcol(256), qcol(SUBLANES), seq,
                  pl.BlockSpec((LANES, S), lambda b, i: (0, b)), seq, _whole(biasT.shape)],
        out_specs=pl.BlockSpec((tq, H * DSA_HEAD_DIM), lambda b, i: (b * nq + i, 0)),
        scratch_shapes=[pltpu.VMEM((LANES, H * tq), BF16), pltpu.VMEM((S, tq), I32), pltpu.VMEM((S, tq), F32),
                        pltpu.VMEM((1, H * tq), F32), pltpu.VMEM((1, H * tq), F32),
                        pltpu.VMEM((LANES, H * tq), F32)],
        compiler_params=_cp("parallel", "arbitrary"),
        name="dsa_prompt",
    )(qcT, qiT, wiT, kcb, vT, kib, biasT)


def _lru_gates(xc, wa, ba, wg, bg, coef):
    xcb = xc.astype(BF16)
    r = jax.nn.sigmoid(_dot(xcb, wa) + ba)
    gi = jax.nn.sigmoid(_dot(xcb, wg) + bg)
    log_a = coef * r
    return jnp.exp(log_a), xc * gi, jnp.sqrt(1.0 - jnp.exp(2.0 * log_a))


def _lru_coef(lam):
    z = -lam
    return -LRU_C * (jnp.maximum(z, 0.0) + jnp.log1p(jnp.exp(-jnp.abs(z))))


def _rglru_prompt_kernel(x_ref, wc_ref, bc_ref, wa_ref, ba_ref, wg_ref, bg_ref, lam_ref, o_ref, hT_ref,
                         xpad_ref, a_ref, u_ref, *, chunk):
    S, W = x_ref.shape
    PAD = SUBLANES
    xpad_ref[0:PAD, :] = jnp.zeros((PAD, W), F32)
    coef = _lru_coef(lam_ref[...])
    wa = wa_ref[...]
    wg = wg_ref[...]
    for c in range(S // chunk):
        r0 = c * chunk
        xpad_ref[PAD + r0:PAD + r0 + chunk, :] = x_ref[r0:r0 + chunk, :]
        xc = bc_ref[...] + x_ref[r0:r0 + chunk, :] * wc_ref[CONV_W - 1:CONV_W, :]
        for k in range(CONV_W - 1):
            sh = CONV_W - 1 - k
            xc = xc + xpad_ref[PAD + r0 - sh:PAD + r0 - sh + chunk, :] * wc_ref[k:k + 1, :]
        a, xg, mult = _lru_gates(xc, wa, ba_ref[...], wg, bg_ref[...], coef)
        if c == 0:
            first = lax.broadcasted_iota(I32, (chunk, W), 0) == 0
            a = jnp.where(first, 0.0, a)
            mult = jnp.where(first, 1.0, mult)
        a_ref[r0:r0 + chunk, :] = a
        u_ref[r0:r0 + chunk, :] = xg * mult

    rowi = lax.broadcasted_iota(I32, (SUBLANES, W), 0)

    def scan8(r, h):
        a8 = a_ref[pl.ds(r, SUBLANES), :]
        u8 = u_ref[pl.ds(r, SUBLANES), :]
        for s in (1, 2, 4):
            keep = rowi >= s
            u8 = jnp.where(keep, a8 * pltpu.roll(u8, s, axis=0) + u8, u8)
            a8 = jnp.where(keep, a8 * pltpu.roll(a8, s, axis=0), a8)
        return u8 + a8 * h

    def body(i, h):
        r = pl.multiple_of(i * 16, 16)
        h0 = scan8(r, h)
        h1 = scan8(r + SUBLANES, h0[SUBLANES - 1:SUBLANES, :])
        o_ref[pl.ds(r, 16), :] = jnp.concatenate([h0, h1], axis=0).astype(o_ref.dtype)
        return h1[SUBLANES - 1:SUBLANES, :]

    hT_ref[...] = lax.fori_loop(0, S // 16, body, jnp.zeros((1, W), F32))


def _rglru_prompt(xr, lw, B, S):
    W = LRU_WIDTH
    vec = _whole((1, W))
    return pl.pallas_call(
        functools.partial(_rglru_prompt_kernel, chunk=256),
        out_shape=[jax.ShapeDtypeStruct((B * S, W), BF16), jax.ShapeDtypeStruct((B, 1, W), F32)],
        grid=(B,),
        in_specs=[pl.BlockSpec((S, W), lambda b: (b, 0)), _whole((CONV_W, W)), vec,
                  _whole((W, W)), vec, _whole((W, W)), vec, vec],
        out_specs=[pl.BlockSpec((S, W), lambda b: (b, 0)), pl.BlockSpec((None, 1, W), lambda b: (b, 0, 0))],
        scratch_shapes=[pltpu.VMEM((S + SUBLANES, W), F32), pltpu.VMEM((S, W), F32), pltpu.VMEM((S, W), F32)],
        compiler_params=_cp("parallel"),
        name="rglru_prompt",
    )(xr, lw["w_conv"], lw["b_conv"], lw["w_rg_a"], lw["b_rg_a"], lw["w_rg_i"], lw["b_rg_i"], lw["lru_lambda"])


def _rglru_sample_kernel(x_ref, cb_ref, h0_ref, wc_ref, bc_ref, wa_ref, ba_ref, wg_ref, bg_ref, lam_ref,
                         o_ref, hT_ref):
    T = x_ref.shape[0]
    xp = [cb_ref[k] for k in range(CONV_W - 1)] + [x_ref[t] for t in range(T)]
    coef = _lru_coef(lam_ref[...])
    h = h0_ref[...]
    for t in range(T):
        xc = bc_ref[...]
        for k in range(CONV_W):
            xc = xc + xp[t + k] * wc_ref[k:k + 1, :]
        a, xg, mult = _lru_gates(xc, wa_ref[...], ba_ref[...], wg_ref[...], bg_ref[...], coef)
        h = a * h + xg * mult
        o_ref[t] = h.astype(o_ref.dtype)
    hT_ref[...] = h


def _rglru_sample(xr_t, conv_t, h0, lw):
    T, Bd, W = xr_t.shape
    return pl.pallas_call(
        _rglru_sample_kernel,
        out_shape=[jax.ShapeDtypeStruct((T, Bd, W), BF16), jax.ShapeDtypeStruct((Bd, W), F32)],
        compiler_params=pltpu.CompilerParams(vmem_limit_bytes=VMEM_LIMIT),
        name="rglru_sample",
    )(xr_t, conv_t, h0, lw["w_conv"], lw["b_conv"], lw["w_rg_a"], lw["b_rg_a"], lw["w_rg_i"], lw["b_rg_i"],
      lw["lru_lambda"])


def _merge_kernel(x_ref, om_ref, ol_ref, od_ref, g_ref, gm_ref, wm_ref, wl_ref, wd_ref, wo_ref, o_ref):
    D = x_ref.shape[1]
    y = (g_ref[:, 0:D].astype(F32) * _dot(om_ref[...], wm_ref[...])
         + g_ref[:, D:2 * D].astype(F32) * _dot(ol_ref[...], wl_ref[...])
         + g_ref[:, 2 * D:3 * D].astype(F32) * _dot(od_ref[...], wd_ref[...]))
    o_ref[...] = x_ref[...] + gm_ref[...] * _dot(y.astype(BF16), wo_ref[...])


def _merge(x, o_mla, o_lru, o_dsa, gates, gate_mix, group_rows, lw):
    N, D = x.shape
    tm = ROW_TILE
    R = gate_mix.shape[1]
    tpg = group_rows // tm
    row = lambda w: pl.BlockSpec((tm, w), lambda i: (i, 0))
    return pl.pallas_call(
        _merge_kernel,
        out_shape=jax.ShapeDtypeStruct((N, D), F32),
        grid=(N // tm,),
        in_specs=[row(D), row(512), row(512), row(512), row(N_BRANCH * D),
                  pl.BlockSpec((None, R, D), lambda i: (i // tpg, 0, 0)),
                  _whole((512, D)), _whole((512, D)), _whole((512, D)), _whole((D, D))],
        out_specs=row(D),
        compiler_params=_cp("parallel"),
        name="merge",
    )(x, o_mla, o_lru, o_dsa, gates, gate_mix, lw["w_br_mla"], lw["w_br_lru"], lw["w_br_dsa"], lw["w_o"])


def _router_kernel(x_ref, gn_ref, sh_ref, sc_ref, wr_ref, br_ref, h_out, e_out, g_out):
    tm = x_ref.shape[0]
    h = _rms(x_ref[...], gn_ref[...]) * (1.0 + sc_ref[...]) + sh_ref[...]
    hb = h.astype(BF16)
    h_out[...] = hb
    logits = _dot(hb, wr_ref[...]) + br_ref[...]
    lane = lax.broadcasted_iota(I32, (tm, LANES), 1)
    lanef = lane.astype(F32)
    vals, idxs = [], []
    for _ in range(TOP_K):
        mx = jnp.max(logits, axis=-1, keepdims=True)
        ix = jnp.min(jnp.where(logits == mx, lanef, float(LANES)), axis=-1, keepdims=True)
        vals.append(mx)
        idxs.append(ix)
        logits = jnp.where(lanef == ix, NEG_INF, logits)
    ex = [jnp.exp(v - vals[0]) for v in vals]
    den = ex[0]
    for e in ex[1:]:
        den = den + e
    earr = jnp.zeros((tm, LANES), F32)
    garr = jnp.zeros((tm, LANES), F32)
    for k in range(TOP_K):
        earr = jnp.where(lane == k, idxs[k], earr)
        garr = jnp.where(lane == k, ex[k] / den, garr)
    e_out[...] = earr[:, 0:TOP_K].astype(I32)
    g_out[...] = garr[:, 0:TOP_K]


def _router(x, shift, scale, group_rows, lw):
    N, D = x.shape
    tm = ROW_TILE
    R = shift.shape[1]
    tpg = group_rows // tm
    row = lambda w: pl.BlockSpec((tm, w), lambda i: (i, 0))
    mod = pl.BlockSpec((None, R, D), lambda i: (i // tpg, 0, 0))
    return pl.pallas_call(
        _router_kernel,
        out_shape=[jax.ShapeDtypeStruct((N, D), BF16), jax.ShapeDtypeStruct((N, TOP_K), I32),
                   jax.ShapeDtypeStruct((N, TOP_K), F32)],
        grid=(N // tm,),
        in_specs=[row(D), _whole((1, D)), mod, mod, _whole((D, LANES)), _whole((1, LANES))],
        out_specs=[row(D), row(TOP_K), row(TOP_K)],
        compiler_params=_cp("parallel"),
        name="router",
    )(x, lw["g_norm_ffn"], shift, scale, lw["w_router"], lw["b_router"])


def _expert_kernel(be_ref, x_ref, wu_ref, bu_ref, wd_ref, bd_ref, o_ref):
    F = wd_ref.shape[0]
    z = _dot(x_ref[...], wu_ref[...].astype(BF16)) + bu_ref[...]
    glu = jnp.minimum(z[:, 0:F], SWIGLU_LIMIT)
    lin = jnp.clip(z[:, F:2 * F], -SWIGLU_LIMIT, SWIGLU_LIMIT)
    act = glu * jax.nn.sigmoid(SWIGLU_ALPHA * glu) * (lin + 1.0)
    o_ref[...] = (_dot(act.astype(BF16), wd_ref[...].astype(BF16)) + bd_ref[...]).astype(o_ref.dtype)


def _experts(xs, blk_e, w_up, b_up, w_down, b_down, layer):
    Mp, D = xs.shape
    depth, E, _, F2 = w_up.shape
    F = F2 // 2
    bm = MOE_BLOCK
    wspec = lambda r, c: pl.BlockSpec((None, None, r, c), lambda j, be: (layer, be[j], 0, 0))
    grid_spec = pltpu.PrefetchScalarGridSpec(
        num_scalar_prefetch=1,
        grid=(Mp // bm,),
        in_specs=[pl.BlockSpec((bm, D), lambda j, be: (j, 0)),
                  wspec(D, F2), wspec(1, F2), wspec(F, D), wspec(1, D)],
        out_specs=pl.BlockSpec((bm, D), lambda j, be: (j, 0)),
    )
    return pl.pallas_call(
        _expert_kernel,
        out_shape=jax.ShapeDtypeStruct((Mp, D), BF16),
        grid_spec=grid_spec,
        compiler_params=_cp("arbitrary"),
        name="experts",
    )(blk_e, xs, w_up, b_up.reshape(depth, E, 1, F2), w_down, b_down.reshape(depth, E, 1, D))


def _combine_kernel(x_ref, y_ref, g_ref, gf_ref, gfin_ref, o_ref, *, final):
    g = g_ref[...]
    y = g[:, 0:1] * y_ref[0].astype(F32)
    for k in range(1, TOP_K):
        y = y + g[:, k:k + 1] * y_ref[k].astype(F32)
    out = x_ref[...] + gf_ref[...] * y
    if final:
        out = _rms(out, gfin_ref[...])
    o_ref[...] = out


def _combine(x, y4, gate, row0, gate_ffn, group_rows, g_final, final):
    N, D = x.shape
    tm = ROW_TILE
    R = gate_ffn.shape[1]
    tpg = group_rows // tm
    t0 = row0 // tm
    row = lambda w: pl.BlockSpec((tm, w), lambda i: (i, 0))
    return pl.pallas_call(
        functools.partial(_combine_kernel, final=final),
        out_shape=jax.ShapeDtypeStruct((N, D), F32),
        grid=(N // tm,),
        in_specs=[row(D), pl.BlockSpec((TOP_K, tm, D), lambda i: (0, i + t0, 0)),
                  pl.BlockSpec((tm, TOP_K), lambda i: (i + t0, 0)),
                  pl.BlockSpec((None, R, D), lambda i: (i // tpg, 0, 0)), _whole((1, D))],
        out_specs=row(D),
        compiler_params=_cp("parallel"),
        name="combine",
    )(x, y4, gate, gate_ffn, g_final)


class _PageStream:
    def __init__(self, cache_ref, buf_ref, sem_ref, layer, along_lanes, seq_mul=1, seq_off=0):
        self.cache_ref, self.buf_ref, self.sem_ref = cache_ref, buf_ref, sem_ref
        self.layer, self.along_lanes = layer, along_lanes
        self.seq_mul, self.seq_off = seq_mul, seq_off

    def copies(self, pt_ref, b, j, slot, npg):
        out = []
        rows = self.cache_ref.shape[2]
        for i in range(npg):
            win = pl.ds(i * LANES, LANES)
            if self.along_lanes:
                dst = self.buf_ref.at[slot, pl.ds(self.seq_off * rows, rows), win]
            else:
                dst = self.buf_ref.at[slot, win, :]
            src = self.cache_ref.at[self.layer, pt_ref[self.seq_mul * b + self.seq_off, j * npg + i]]
            out.append(pltpu.make_async_copy(src, dst, self.sem_ref.at[slot]))
        return out


def _stream_step(streams, pt_ref, nkt, npg):
    s = pl.program_id(0)
    slot = lax.rem(s, 2)

    def each(step, sl, fn):
        for st in streams:
            for c in st.copies(pt_ref, step // nkt, lax.rem(step, nkt), sl, npg):
                fn(c)

    @pl.when(s == 0)
    def _():
        each(s, slot, lambda c: c.start())

    @pl.when(s + 1 < pl.num_programs(0))
    def _():
        each(s + 1, 1 - slot, lambda c: c.start())

    each(s, slot, lambda c: c.wait())
    return s // nkt, lax.rem(s, nkt), slot


def _online_update(s, pv, m_ref, l_ref, acc_ref):
    m_old = m_ref[...]
    m_new = jnp.maximum(m_old, jnp.max(s, axis=-1, keepdims=True))
    m_use = jnp.where(m_new == NEG_INF, 0.0, m_new)
    alpha = jnp.exp(m_old - m_use)
    p = jnp.exp(s - m_use)
    l_ref[...] = alpha * l_ref[...] + jnp.sum(p, axis=-1, keepdims=True)
    acc_ref[...] = alpha * acc_ref[...] + pv(p.astype(BF16))
    m_ref[...] = m_new


def _init_softmax(m_ref, l_ref, acc_ref):
    m_ref[...] = jnp.full(m_ref.shape, NEG_INF, F32)
    l_ref[...] = jnp.zeros(l_ref.shape, F32)
    acc_ref[...] = jnp.zeros(acc_ref.shape, F32)


def _sample_grid(page_table):
    Bd, n_pages = page_table.shape
    npg = min(PAGES_PER_STEP, n_pages)
    nkt = n_pages // npg
    return Bd, npg, nkt, npg * LANES


def _mla_sample_kernel(pt_ref, qa_ref, qp_ref, lat_hbm, kpeT_hbm, latn_ref, kpenT_ref, o_ref,
                       lat_buf, kpe_buf, sem, m_ref, l_ref, acc_ref, *, layer, T, nkt, npg):
    streams = [_PageStream(lat_hbm, lat_buf, sem.at[0], layer, False),
               _PageStream(kpeT_hbm, kpe_buf, sem.at[1], layer, True)]
    b, j, slot = _stream_step(streams, pt_ref, nkt, npg)

    @pl.when(j == 0)
    def _():
        _init_softmax(m_ref, l_ref, acc_ref)

    lat = lat_buf[slot].astype(BF16)
    s = _dot_nt(qa_ref[...], lat) + _dot(qp_ref[...], kpe_buf[slot].astype(BF16))
    _online_update(s, lambda p: _dot(p, lat), m_ref, l_ref, acc_ref)

    @pl.when(j == nkt - 1)
    def _():
        R, C = qa_ref.shape[0], latn_ref.shape[0]
        latn = latn_ref[...].astype(BF16)
        sn = _dot_nt(qa_ref[...], latn) + _dot(qp_ref[...], kpenT_ref[...].astype(BF16))
        t = lax.rem(lax.broadcasted_iota(I32, (R, C), 0), T)
        c = lax.broadcasted_iota(I32, (R, C), 1)
        _online_update(jnp.where(c <= t, sn, NEG_INF), lambda p: _dot(p, latn), m_ref, l_ref, acc_ref)
        o_ref[...] = acc_ref[...] / l_ref[...]


def _mla_sample(page_table, qa, qp, cache_lat, cache_kpeT, layer, latn, kpenT, T):
    Bd, npg, nkt, tk = _sample_grid(page_table)
    _, R, C = qa.shape
    NP = latn.shape[1]
    per_b = lambda r, w: pl.BlockSpec((None, r, w), lambda s, pt: (s // nkt, 0, 0))
    hbm = pl.BlockSpec(memory_space=pl.ANY)
    grid_spec = pltpu.PrefetchScalarGridSpec(
        num_scalar_prefetch=1,
        grid=(Bd * nkt,),
        in_specs=[per_b(R, C), per_b(R, MLA_ROPE), hbm, hbm, per_b(NP, C), per_b(MLA_ROPE, NP)],
        out_specs=per_b(R, C),
        scratch_shapes=[pltpu.VMEM((2, tk, C), F32), pltpu.VMEM((2, MLA_ROPE, tk), F32),
                        pltpu.SemaphoreType.DMA((2, 2)),
                        pltpu.VMEM((R, 1), F32), pltpu.VMEM((R, 1), F32), pltpu.VMEM((R, C), F32)],
    )
    return pl.pallas_call(
        functools.partial(_mla_sample_kernel, layer=layer, T=T, nkt=nkt, npg=npg),
        out_shape=jax.ShapeDtypeStruct((Bd, R, C), F32),
        grid_spec=grid_spec,
        compiler_params=_cp("arbitrary"),
        name="mla_sample",
    )(page_table, qa, qp, cache_lat, cache_kpeT, latn, kpenT)


def _idx_sample_kernel(pt_ref, qi_ref, wi_ref, kiT_hbm, kinT_ref, am_ref, ki_buf, sem, keys_ref,
                       *, layer, T, nb, nkt, npg, n_sel):
    streams = [_PageStream(kiT_hbm, ki_buf, sem.at[k], layer, True, nb, k) for k in range(nb)]
    b, j, slot = _stream_step(streams, pt_ref, nkt, npg)
    tk = npg * LANES

    def scores(k, kiT):
        d = jnp.maximum(_dot(qi_ref[k], kiT), 0.0)
        wi = wi_ref[k]
        sc = d[0:T] * wi[:, 0:1]
        for hh in range(1, IDX_HEADS):
            sc = sc + d[hh * T:(hh + 1) * T] * wi[:, hh:hh + 1]
        return sc

    def keys_of(page_of):
        return _sort_key(jnp.concatenate([scores(k, page_of(k).astype(BF16)) for k in range(nb)], axis=0))

    keys_ref[:, pl.ds(pl.multiple_of(j * tk, tk), tk)] = keys_of(
        lambda k: ki_buf[slot, k * IDX_DIM:(k + 1) * IDX_DIM, :])

    @pl.when(j == nkt - 1)
    def _():
        tn = lax.rem(lax.broadcasted_iota(I32, (nb * T, LANES), 0), T)
        cn = lax.broadcasted_iota(I32, (nb * T, LANES), 1)
        keys_ref[:, nkt * tk:] = jnp.where(cn <= tn, keys_of(lambda k: kinT_ref[k]), KEY_NEG)
        _select_rows(keys_ref, am_ref, n_sel)


def _idx_sample(page_table, qi, wi, cache_kiT, layer, kinT, T, n_sel):
    Bd, npg, nkt, tk = _sample_grid(page_table)
    assert SUBLANES % T == 0
    nb = SUBLANES // T
    assert Bd % nb == 0
    NP = kinT.shape[2]
    assert NP == LANES
    width = nkt * tk + NP
    per_b = lambda r, w: pl.BlockSpec((nb, r, w), lambda s, pt: (s // nkt, 0, 0))
    grid_spec = pltpu.PrefetchScalarGridSpec(
        num_scalar_prefetch=1,
        grid=(Bd // nb * nkt,),
        in_specs=[per_b(IDX_HEADS * T, IDX_DIM), per_b(T, LANES), pl.BlockSpec(memory_space=pl.ANY),
                  per_b(IDX_DIM, NP)],
        out_specs=pl.BlockSpec((None, nb * T, width), lambda s, pt: (s // nkt, 0, 0)),
        scratch_shapes=[pltpu.VMEM((2, nb * IDX_DIM, tk), F32), pltpu.SemaphoreType.DMA((nb, 2)),
                        pltpu.VMEM((nb * T, width), I32)],
    )
    am = pl.pallas_call(
        functools.partial(_idx_sample_kernel, layer=layer, T=T, nb=nb, nkt=nkt, npg=npg, n_sel=n_sel),
        out_shape=jax.ShapeDtypeStruct((Bd // nb, nb * T, width), F32),
        grid_spec=grid_spec,
        compiler_params=_cp("arbitrary"),
        name="idx_sample",
    )(page_table, qi, wi, cache_kiT, kinT)
    return am.reshape(Bd, T, width)


def _dsa_sample_kernel(pt_ref, q_ref, kT_hbm, vT_hbm, knT_ref, vnT_ref, am_ref, amn_ref, b_ref, bn_ref, o_ref,
                       k_buf, v_buf, sem, m_ref, l_ref, acc_ref, *, layer, T, nkt, npg):
    streams = [_PageStream(kT_hbm, k_buf, sem.at[0], layer, True),
               _PageStream(vT_hbm, v_buf, sem.at[1], layer, True)]
    b, j, slot = _stream_step(streams, pt_ref, nkt, npg)
    H = DSA_HEADS

    @pl.when(j == 0)
    def _():
        _init_softmax(m_ref, l_ref, acc_ref)

    def rows_of(am):
        w = am.shape[1]
        return jnp.concatenate([jnp.broadcast_to(am[t:t + 1, :], (H, w)) for t in range(T)], axis=0)

    s = _dot(q_ref[...], k_buf[slot].astype(BF16)) + b_ref[...] + rows_of(am_ref[...])
    _online_update(s, lambda p: _dot_nt(p, v_buf[slot].astype(BF16)), m_ref, l_ref, acc_ref)

    @pl.when(j == nkt - 1)
    def _():
        sn = _dot(q_ref[...], knT_ref[...].astype(BF16)) + bn_ref[...] + rows_of(amn_ref[...])
        _online_update(sn, lambda p: _dot_nt(p, vnT_ref[...].astype(BF16)), m_ref, l_ref, acc_ref)
        o_ref[...] = acc_ref[...] / l_ref[...]


def _dsa_sample(page_table, q, cache_kT, cache_vT, layer, knT, vnT, am, bias, T):
    Bd, npg, nkt, tk = _sample_grid(page_table)
    _, R, C = q.shape
    NP = knT.shape[2]
    TP = am.shape[1]
    last = nkt * tk // NP
    per_b = lambda r, w: pl.BlockSpec((None, r, w), lambda s, pt: (s // nkt, 0, 0))
    hbm = pl.BlockSpec(memory_space=pl.ANY)
    grid_spec = pltpu.PrefetchScalarGridSpec(
        num_scalar_prefetch=1,
        grid=(Bd * nkt,),
        in_specs=[per_b(R, C), hbm, hbm, per_b(C, NP), per_b(C, NP),
                  pl.BlockSpec((None, TP, tk), lambda s, pt: (s // nkt, 0, lax.rem(s, nkt))),
                  pl.BlockSpec((None, TP, NP), lambda s, pt: (s // nkt, 0, last)),
                  pl.BlockSpec((R, tk), lambda s, pt: (0, lax.rem(s, nkt))),
                  pl.BlockSpec((R, NP), lambda s, pt: (0, last))],
        out_specs=per_b(R, C),
        scratch_shapes=[pltpu.VMEM((2, C, tk), F32), pltpu.VMEM((2, C, tk), F32),
                        pltpu.SemaphoreType.DMA((2, 2)),
                        pltpu.VMEM((R, 1), F32), pltpu.VMEM((R, 1), F32), pltpu.VMEM((R, C), F32)],
    )
    return pl.pallas_call(
        functools.partial(_dsa_sample_kernel, layer=layer, T=T, nkt=nkt, npg=npg),
        out_shape=jax.ShapeDtypeStruct((Bd, R, C), F32),
        grid_spec=grid_spec,
        compiler_params=_cp("arbitrary"),
        name="dsa_sample",
    )(page_table, q, cache_kT, cache_vT, knT, vnT, am, am, bias, bias)


def _rope_tables(pos):
    half = MLA_ROPE // 2
    freq = ROPE_THETA ** (-np.arange(half, dtype=np.float64) / half)
    ang = np.asarray(pos, np.float64)[:, None] * freq[None, :]
    cos = np.concatenate([np.cos(ang), np.cos(ang)], axis=1)
    sin = np.concatenate([-np.sin(ang), np.sin(ang)], axis=1)
    cos2 = np.tile(cos, (1, 2 * LANES // MLA_ROPE)).astype(np.float32)
    sin2 = np.tile(sin, (1, 2 * LANES // MLA_ROPE)).astype(np.float32)
    return (jnp.asarray(cos2[:, :LANES]), jnp.asarray(sin2[:, :LANES]),
            jnp.asarray(np.ascontiguousarray(cos2.T)), jnp.asarray(np.ascontiguousarray(sin2.T)))


def _block_diag(w):
    n, c, d = w.shape
    eye = jnp.eye(n, dtype=w.dtype)
    return (eye[:, None, :, None] * w[:, :, None, :]).reshape(n * c, n * d)


def _prep_layer(l, p):
    D = p["w_in"].shape[1]
    w = p["w_in"][l]
    sizes = (MLA_Q_LORA, MLA_KV_LORA, MLA_ROPE, LRU_WIDTH, DSA_HEADS * DSA_HEAD_DIM,
             DSA_KV_HEADS * DSA_HEAD_DIM, DSA_KV_HEADS * DSA_HEAD_DIM, IDX_HEADS * IDX_DIM, IDX_DIM,
             IDX_HEADS, N_BRANCH * D)
    offs = np.cumsum(sizes)[:-1].tolist()
    q_lat, kv_lat, kpe, xr, q_c, k_c, v_c, q_i, k_i, w_i, g = jnp.split(w, offs, axis=1)
    half = MLA_ROPE // 2
    kpe_sw = jnp.concatenate([kpe[:, half:], kpe[:, :half]], axis=1)
    padto = lambda a, n: jnp.pad(a, ((0, 0), (0, n - a.shape[1])))
    w_in = jnp.concatenate([kv_lat, jnp.tile(kpe, (1, ROPE_SLOT)), jnp.tile(kpe_sw, (1, ROPE_SLOT)),
                            xr, k_c, v_c, padto(k_i, LANES), g], axis=1).astype(BF16)
    w_inT = jnp.concatenate([q_lat, kv_lat, q_c, q_i, v_c, padto(w_i, SUBLANES)], axis=1).T.astype(BF16)
    assert w_in.shape[1] == _C_G + N_BRANCH * D and w_inT.shape[0] == _R_END
    w_uq = p["w_uq"][l]
    rope = w_uq[:, :, MLA_NOPE:]
    rope_sw = jnp.concatenate([rope[:, :, half:], rope[:, :, :half]], axis=2)
    row = lambda a: a.reshape(1, -1)
    col = lambda a: jnp.broadcast_to(a.reshape(-1, 1), (a.shape[0], LANES))
    E = p["w_router"].shape[2]
    return {
        "w_in": w_in, "w_inT": w_inT,
        "g_norm_mix": row(p["g_norm_mix"][l]), "g_norm_ffn": row(p["g_norm_ffn"][l]),
        "gqT": col(p["g_q_norm"][l]), "g_kv_norm": row(p["g_kv_norm"][l]), "gkvT": col(p["g_kv_norm"][l]),
        "wnT": jnp.transpose(w_uq[:, :, :MLA_NOPE], (1, 2, 0)).astype(BF16),
        "wrT": rope.reshape(MLA_Q_LORA, -1).T.astype(BF16),
        "wrsT": rope_sw.reshape(MLA_Q_LORA, -1).T.astype(BF16),
        "wukT": jnp.transpose(p["w_uk"][l], (1, 0, 2)).astype(BF16),
        "wuvT": jnp.transpose(p["w_uv"][l], (1, 2, 0)).astype(BF16),
        "bd_full": _block_diag(jnp.transpose(p["w_uv"][l], (1, 0, 2))).astype(BF16),
        "w_conv": p["w_conv"][l], "b_conv": row(p["b_conv"][l]),
        "w_rg_a": _block_diag(p["w_rg_a"][l]).astype(BF16), "b_rg_a": row(p["b_rg_a"][l]),
        "w_rg_i": _block_diag(p["w_rg_i"][l]).astype(BF16), "b_rg_i": row(p["b_rg_i"][l]),
        "lru_lambda": row(p["lru_lambda"][l]),
        "w_br_mla": p["w_br_mla"][l].astype(BF16), "w_br_lru": p["w_br_lru"][l].astype(BF16),
        "w_br_dsa": p["w_br_dsa"][l].astype(BF16), "w_o": p["w_o"][l].astype(BF16),
        "w_router": jnp.pad(p["w_router"][l], ((0, 0), (0, LANES - E))).astype(BF16),
        "b_router": jnp.pad(row(p["b_router"][l]), ((0, 0), (0, LANES - E)), constant_values=NEG_INF),
    }


def _moe(h_all, eidx, w_up, b_up, w_down, b_down, layer):
    N, D = h_all.shape
    E = w_up.shape[1]
    bm = MOE_BLOCK
    M = N * TOP_K
    flat_e = eidx.reshape(M)
    order = jnp.argsort(flat_e)
    e_sorted = flat_e[order]
    counts = jnp.bincount(flat_e, length=E)
    padded = (counts + bm - 1) // bm * bm
    pad_end = jnp.cumsum(padded)
    pad_start = pad_end - padded
    start = jnp.cumsum(counts) - counts
    dest = pad_start[e_sorted] + jnp.arange(M) - start[e_sorted]
    n_blocks = M // bm + E
    blk_start = jnp.arange(n_blocks) * bm
    blk_e = jnp.minimum(jnp.sum(pad_end[None, :] <= blk_start[:, None], axis=1), E - 1).astype(I32)
    row_e = jnp.repeat(blk_e, bm)
    within = jnp.arange(n_blocks * bm) - pad_start[row_e]
    src = jnp.clip(start[row_e] + within, 0, M - 1)
    tok = jnp.where(within < counts[row_e], order[src] // TOP_K, 0).astype(I32)
    pos = dest[jnp.argsort(order)].astype(I32)
    xs = h_all[tok]
    out = _experts(xs, blk_e, w_up, b_up, w_down, b_down, layer)
    return out[pos.reshape(N, TOP_K).T]


def kernel(x_prompt, x_sample, cache_mla_latent, cache_mla_kpe, cache_dsa_k, cache_dsa_v, cache_idx_k, state_lru_h, state_conv, page_table, c_prompt, c_sample, rel_bias, g_norm_mix, g_norm_ffn, w_ada, b_ada, w_in, g_q_norm, w_uq, g_kv_norm, w_uk, w_uv, w_conv, b_conv, w_rg_a, b_rg_a, w_rg_i, b_rg_i, lru_lambda, w_br_mla, w_br_lru, w_br_dsa, w_o, w_router, b_router, w_up, b_up, w_down, b_down, g_final):
    params = dict(g_norm_mix=g_norm_mix, g_norm_ffn=g_norm_ffn, w_in=w_in, g_q_norm=g_q_norm, w_uq=w_uq,
                  g_kv_norm=g_kv_norm, w_uk=w_uk, w_uv=w_uv, w_conv=w_conv, b_conv=b_conv, w_rg_a=w_rg_a,
                  b_rg_a=b_rg_a, w_rg_i=w_rg_i, b_rg_i=b_rg_i, lru_lambda=lru_lambda, w_br_mla=w_br_mla,
                  w_br_lru=w_br_lru, w_br_dsa=w_br_dsa, w_o=w_o, w_router=w_router, b_router=b_router)
    B, S, D = x_prompt.shape
    Bd, T, _ = x_sample.shape
    depth = w_in.shape[0]
    n_pages, page = page_table.shape[1], cache_mla_latent.shape[2]
    past = n_pages * page
    Np, Ns = B * S, Bd * T
    tm = ROW_TILE
    TP = SUBLANES
    NP = LANES
    assert page == LANES and S % max(tm, ATT_TK, MLA_TK) == 0 and Ns % tm == 0 and T <= TP
    assert n_pages % min(PAGES_PER_STEP, n_pages) == 0 and tm % T == 0
    H = MLA_HEADS

    rope_p = _rope_tables(np.arange(S))
    rope_s = _rope_tables(np.tile(past + np.arange(T), tm // T))
    ntypes = -(-(_BUCKET_FAR + ATT_TK - 1) // ATT_TQ) + 1
    bias_p = _bias_tiles(rel_bias, ntypes, ATT_TK, ATT_TQ, 0, ATT_TQ, -1, 1, "bias_prompt")
    bias_s = _bias_tiles(rel_bias, T, SUBLANES, past + NP, past, 1, 0, -1, "bias_sample")
    bias_s = bias_s[:, :, 0, :].reshape(T * DSA_HEADS, past + NP)
    n_sel_s = min(DSA_TOPK_MAX, (past + T) // 4)
    feat_major = lambda c: jnp.swapaxes(c.reshape(c.shape[:3] + (-1,)), 2, 3)
    cache_kpeT, cache_kiT = feat_major(cache_mla_kpe), feat_major(cache_idx_k)
    cache_kT, cache_vT = feat_major(cache_dsa_k), feat_major(cache_dsa_v)

    xp = x_prompt.reshape(Np, D)
    xs = x_sample.reshape(Ns, D)
    c_all = jnp.concatenate([c_prompt, c_sample], axis=0)
    new_p = [[] for _ in range(7)]
    new_s = [[] for _ in range(7)]
    per_tok = lambda a: jnp.repeat(a, T, axis=0).reshape(Ns // tm, tm, D)
    for l in range(depth):
        lw = _prep_layer(l, params)
        ada = _matmul(c_all, w_ada[l], b_ada[l].reshape(1, -1), D, "ada")
        ada_p = [a.reshape(B, 1, D) for a in jnp.split(ada[:B], 6, axis=1)]
        ada_s = [per_tok(a) for a in jnp.split(ada[B:], 6, axis=1)]

        (qT, ckv, kp, ckvT, kpe, xr, qcT, kc, vc, kcb, vT, qiT, ki, kib, wiT, gates) = _in_proj(
            xp, ada_p[0], ada_p[1], S, rope_p, lw)
        o_mla = _mla_prompt(qT, kp, ckvT, lw["wuvT"], B, S)
        o_lru, hT = _rglru_prompt(xr, lw, B, S)
        o_dsa = _dsa_prompt(qcT, qiT, wiT, kcb, vT, kib, bias_p, B, S)
        xp = _merge(xp, o_mla, o_lru, o_dsa, gates, ada_p[2], S, lw)
        xr3 = xr.reshape(B, S, LRU_WIDTH)
        conv_p = jnp.concatenate([jnp.zeros((B, CONV_W - 1, LRU_WIDTH), F32), xr3], axis=1)[:, S:]
        for i, a in enumerate((ckv.reshape(B, S, -1), kpe[:, :MLA_ROPE].reshape(B, S, -1),
                               kc.reshape(B, S, DSA_KV_HEADS, DSA_HEAD_DIM),
                               vc.reshape(B, S, DSA_KV_HEADS, DSA_HEAD_DIM),
                               ki[:, :IDX_DIM].reshape(B, S, -1), hT.reshape(B, -1), conv_p)):
            new_p[i].append(a)

        (qT, ckv, kp, ckvT, kpe, xr, qcT, kc, vc, kcb, vT, qiT, ki, kib, wiT, gates) = _in_proj(
            xs, ada_s[0], ada_s[1], tm, rope_s, lw)
        q4 = jnp.transpose(qT.reshape(QK_W, Ns // LANES, H, LANES), (2, 1, 3, 0)).reshape(H, Bd, T, QK_W)
        q4 = jnp.transpose(q4, (1, 0, 2, 3))
        qa = q4[..., :MLA_KV_LORA].reshape(Bd, H * T, MLA_KV_LORA)
        qp = jnp.stack([q4[:, h, :, MLA_KV_LORA + (h % ROPE_SLOT) * MLA_ROPE:
                           MLA_KV_LORA + (h % ROPE_SLOT + 1) * MLA_ROPE] for h in range(H)], axis=1)
        qp = qp.reshape(Bd, H * T, MLA_ROPE)
        padk = lambda a: jnp.pad(a.reshape(Bd, T, -1), ((0, 0), (0, NP - T), (0, 0)))
        padkT = lambda a: jnp.swapaxes(padk(a), 1, 2)
        o_lat = _mla_sample(page_table, qa, qp, cache_mla_latent, cache_kpeT, l,
                            padk(ckv), padkT(kpe[:, :MLA_ROPE]), T)
        o_lat = jnp.transpose(o_lat.reshape(Bd, H, T, MLA_KV_LORA), (0, 2, 1, 3)).reshape(Ns, H * MLA_KV_LORA)
        o_mla = _matmul(o_lat, lw["bd_full"], jnp.zeros((1, H * MLA_V), F32), H * MLA_V,
                        "mla_sample_out").astype(BF16)

        xr_t = jnp.transpose(xr.reshape(Bd, T, LRU_WIDTH), (1, 0, 2))
        o_lru_t, hT = _rglru_sample(xr_t, jnp.transpose(state_conv[l], (1, 0, 2)), state_lru_h[l], lw)
        o_lru = jnp.transpose(o_lru_t, (1, 0, 2)).reshape(Ns, LRU_WIDTH)
        conv_s = jnp.concatenate([state_conv[l], xr.reshape(Bd, T, LRU_WIDTH)], axis=1)[:, T:]

        qi_s = jnp.transpose(qiT.reshape(IDX_HEADS, IDX_DIM, Bd, T), (2, 0, 3, 1))
        qi_s = qi_s.reshape(Bd, IDX_HEADS * T, IDX_DIM)
        wi_s = jnp.transpose(wiT.reshape(SUBLANES, Bd, T), (1, 2, 0))
        wi_s = jnp.pad(wi_s, ((0, 0), (0, 0), (0, LANES - SUBLANES)))
        am = _idx_sample(page_table, qi_s, wi_s, cache_kiT, l, padkT(ki[:, :IDX_DIM]), T, n_sel_s)
        qc4 = jnp.transpose(qcT.reshape(DSA_KV_HEADS, DSA_GROUP, DSA_HEAD_DIM, Bd, T), (3, 4, 0, 1, 2))
        zeros = jnp.zeros_like(qc4)
        q_s = jnp.stack([jnp.concatenate([qc4[:, :, 0], zeros[:, :, 0]], axis=-1),
                         jnp.concatenate([zeros[:, :, 1], qc4[:, :, 1]], axis=-1)], axis=2)
        q_s = q_s.reshape(Bd, T * DSA_HEADS, DSA_KV_HEADS * DSA_HEAD_DIM)
        o_s = _dsa_sample(page_table, q_s, cache_kT, cache_vT, l, padkT(kc), padkT(vc), am, bias_s, T)
        o_s = o_s.reshape(Bd, T, DSA_KV_HEADS, DSA_GROUP, DSA_KV_HEADS, DSA_HEAD_DIM)
        o_dsa = jnp.stack([o_s[:, :, g, :, g] for g in range(DSA_KV_HEADS)], axis=2)
        o_dsa = o_dsa.reshape(Ns, DSA_HEADS * DSA_HEAD_DIM).astype(BF16)
        xs = _merge(xs, o_mla, o_lru, o_dsa, gates, ada_s[2], tm, lw)
        for i, a in enumerate((ckv.reshape(Bd, T, -1), kpe[:, :MLA_ROPE].reshape(Bd, T, -1),
                               kc.reshape(Bd, T, DSA_KV_HEADS, DSA_HEAD_DIM),
                               vc.reshape(Bd, T, DSA_KV_HEADS, DSA_HEAD_DIM),
                               ki[:, :IDX_DIM].reshape(Bd, T, -1), hT, conv_s)):
            new_s[i].append(a)

        h_p, e_p, g_p = _router(xp, ada_p[3], ada_p[4], S, lw)
        h_s, e_s, g_s = _router(xs, ada_s[3], ada_s[4], tm, lw)
        gate = jnp.concatenate([g_p, g_s], axis=0)
        y4 = _moe(jnp.concatenate([h_p, h_s], axis=0), jnp.concatenate([e_p, e_s], axis=0),
                  w_up, b_up, w_down, b_down, l)
        final = l == depth - 1
        gfin = g_final.reshape(1, D)
        xp = _combine(xp, y4, gate, 0, ada_p[5], S, gfin, final)
        xs = _combine(xs, y4, gate, Np, ada_s[5], tm, gfin, final)

    outs = [xp.reshape(B, S, D), xs.reshape(Bd, T, D)]
    outs += [jnp.stack(a) for a in new_p] + [jnp.stack(a) for a in new_s]
    return tuple(outs)
```
